```python
import math
import jax, jax.numpy as jnp
from jax import lax
import numpy as np

D_MODEL = 1024
BATCH = 8
SEQ = 2048
DEPTH = 1
DEC_BATCH = 32
DEC_SEQ = 1
PAST_LEN = 8192
PAGE_SIZE = 128

HEAD_DIM = 64
N_HEADS = D_MODEL // HEAD_DIM
MIX_WIDTH = N_HEADS * HEAD_DIM
NSA_HEADS = N_HEADS // 2
SB_HEADS = N_HEADS - NSA_HEADS
NSA_KV_HEADS = 2
NSA_GROUP = NSA_HEADS // NSA_KV_HEADS
CMP_BLOCK = 32
CMP_STRIDE = 16
CMP_HIDDEN = 2 * HEAD_DIM
SEL_BLOCK = 64
N_SEL = 16
N_LOCAL_FORCED = 2
WINDOW = 512
Q_BLOCK = 128
SEL_Q_BLOCK = 64
D_FF = ((8 * D_MODEL + 3 * 256 - 1) // (3 * 256)) * 256
ROPE_THETA = 10000.0
EPS = 1e-6
NEG = -1e30

Q_NSA_W = NSA_HEADS * HEAD_DIM
KV_NSA_W = 3 * 2 * NSA_KV_HEADS * HEAD_DIM
GATE_W = 3 * NSA_HEADS
SB_W = 3 * SB_HEADS * HEAD_DIM
OFF_KV = Q_NSA_W
OFF_GATE = OFF_KV + KV_NSA_W
OFF_SB = OFF_GATE + GATE_W
PROJ_W = OFF_SB + SB_W

kernel_name = 'hymba_nsa_stickbreaking_decode_step'


def rms_norm(x, g):
    x32 = x.astype(jnp.float32)
    y = x32 * lax.rsqrt(jnp.mean(x32 * x32, axis=-1, keepdims=True) + EPS)
    return (y * g.astype(jnp.float32)).astype(x.dtype)


def rope(x, pos):
    half = HEAD_DIM // 2
    inv = jnp.exp(-math.log(ROPE_THETA) * jnp.arange(0, HEAD_DIM, 2, dtype=jnp.float32) / HEAD_DIM)
    ang = pos.astype(jnp.float32)[:, None] * inv[None, :]
    shape = (ang.shape[0],) + (1,) * (x.ndim - 3) + (half,)
    cos, sin = jnp.cos(ang).reshape(shape), jnp.sin(ang).reshape(shape)
    x32 = x.astype(jnp.float32)
    x1, x2 = x32[..., :half], x32[..., half:]
    return jnp.concatenate([x1 * cos - x2 * sin, x1 * sin + x2 * cos], axis=-1).astype(x.dtype)


def project(xn, pos, w_in, g_q, g_k):
    B, T, _ = xn.shape
    p = xn @ w_in
    q = p[..., :OFF_KV].reshape(B, T, NSA_HEADS, HEAD_DIM)
    kv = p[..., OFF_KV:OFF_GATE].reshape(B, T, 3, 2, NSA_KV_HEADS, HEAD_DIM)
    gates = jax.nn.sigmoid(p[..., OFF_GATE:OFF_SB]).reshape(B, T, NSA_HEADS, 3)
    sb = p[..., OFF_SB:].reshape(B, T, 3, SB_HEADS, HEAD_DIM)
    q = rope(rms_norm(q, g_q), pos)
    k = rope(rms_norm(kv[:, :, :, 0], g_k[:, None, :]), pos)
    v = kv[:, :, :, 1]
    nsa_rows = jnp.stack([k[:, :, 0], v[:, :, 0], k[:, :, 1], v[:, :, 1]], axis=2)
    win_rows = jnp.stack([k[:, :, 2], v[:, :, 2]], axis=2)
    return q, gates, nsa_rows, win_rows, sb[:, :, 0], sb[:, :, 1:]


def query_blocked(fn, q, q_pos, block):
    B, T = q.shape[:2]
    if T <= block or T % block:
        return fn(q, q_pos)
    n = T // block
    qb = q.reshape((B, n, block) + q.shape[2:]).swapaxes(0, 1)
    out = lax.map(lambda a: fn(a[0], a[1]), (qb, q_pos.reshape(n, block)))
    return jax.tree_util.tree_map(lambda o: o.swapaxes(0, 1).reshape((B, T) + o.shape[3:]), out)


def compress(rows, w1, w2, pe):
    B, L = rows.shape[:2]
    nc = (L - CMP_BLOCK) // CMP_STRIDE + 1
    idx = jnp.arange(nc)[:, None] * CMP_STRIDE + jnp.arange(CMP_BLOCK)[None, :]
    blk = rows[:, idx] + pe[:, None, :]
    blk = blk.transpose(0, 1, 3, 2, 4).reshape(B, nc, NSA_KV_HEADS, CMP_BLOCK * HEAD_DIM)
    return jax.nn.gelu(blk @ w1) @ w2


def to_blocks(rows, nb):
    B, L = rows.shape[:2]
    rows = jnp.pad(rows, ((0, 0), (0, nb * SEL_BLOCK - L), (0, 0), (0, 0)))
    return rows.reshape(B, nb, SEL_BLOCK, NSA_KV_HEADS, HEAD_DIM).transpose(0, 3, 1, 2, 4)


def overlap_matrix(nc, nb):
    c = jnp.arange(nc)[:, None] * CMP_STRIDE
    j = jnp.arange(nb)[None, :] * SEL_BLOCK
    return ((c < j + SEL_BLOCK) & (c + CMP_BLOCK > j)).astype(jnp.float32)


def nsa_query_block(q, q_pos, kc, vc, c_end, kb, vb, overlap):
    B, Tq = q.shape[:2]
    nb = kb.shape[2]
    scale = HEAD_DIM ** -0.5
    qg = q.reshape(B, Tq, NSA_KV_HEADS, NSA_GROUP, HEAD_DIM)
    s = jnp.einsum('btgqd,bcgd->bgqtc', qg, kc).astype(jnp.float32) * scale
    cmask = c_end[None, :] <= q_pos[:, None]
    p = jax.nn.softmax(jnp.where(cmask, s, NEG), axis=-1) * cmask
    o_cmp = jnp.einsum('bgqtc,bcgd->btgqd', p.astype(vc.dtype), vc)
    imp = jnp.einsum('bgqtc,cj->bgtj', p, overlap)
    qblk = q_pos // SEL_BLOCK
    j = jnp.arange(nb)
    future = j[None, :] > qblk[:, None]
    forced = (j[None, :] == 0) | (j[None, :] > qblk[:, None] - N_LOCAL_FORCED)
    score = jnp.where(future, -jnp.inf, jnp.where(forced, jnp.inf, imp))
    top_val, top_idx = lax.top_k(score, min(N_SEL, nb))
    valid = top_val > -jnp.inf
    gather = jax.vmap(jax.vmap(lambda blocks, ix: blocks[ix]))
    ks = gather(kb, top_idx)
    vs = gather(vb, top_idx)
    ss = jnp.einsum('btgqd,bgtksd->bgqtks', qg, ks).astype(jnp.float32) * scale
    kpos = top_idx[..., None] * SEL_BLOCK + jnp.arange(SEL_BLOCK)
    smask = ((kpos <= q_pos[None, None, :, None, None]) & valid[..., None])[:, :, None]
    ss = jnp.where(smask, ss, NEG)
    nk = ss.shape[-2]
    ps = jax.nn.softmax(ss.reshape(ss.shape[:-2] + (nk * SEL_BLOCK,)), axis=-1).reshape(ss.shape)
    o_slc = jnp.einsum('bgqtks,bgtksd->btgqd', ps.astype(vs.dtype), vs)
    return (o_cmp.reshape(B, Tq, NSA_HEADS, HEAD_DIM), o_slc.reshape(B, Tq, NSA_HEADS, HEAD_DIM))


def nsa_context(q, q_pos, rows, w_ck1, w_ck2, pe_ck, w_cv1, w_cv2, pe_cv):
    L = rows.shape[1]
    kc = compress(rows[:, :, 0], w_ck1, w_ck2, pe_ck)
    vc = compress(rows[:, :, 1], w_cv1, w_cv2, pe_cv)
    nc = kc.shape[1]
    c_end = jnp.arange(nc) * CMP_STRIDE + (CMP_BLOCK - 1)
    nb = -(-L // SEL_BLOCK)
    kb = to_blocks(rows[:, :, 2], nb)
    vb = to_blocks(rows[:, :, 3], nb)
    ov = overlap_matrix(nc, nb)
    return query_blocked(lambda qq, pp: nsa_query_block(qq, pp, kc, vc, c_end, kb, vb, ov), q, q_pos, SEL_Q_BLOCK)


def window_attend(q, q_pos, k, v, k_pos):
    B, Tq = q.shape[:2]
    qg = q.reshape(B, Tq, NSA_KV_HEADS, NSA_GROUP, HEAD_DIM)
    s = jnp.einsum('btgqd,bsgd->bgqts', qg, k).astype(jnp.float32) * HEAD_DIM ** -0.5
    kp, qp = k_pos[None, :], q_pos[:, None]
    mask = (kp <= qp) & (kp >= qp - WINDOW) & (kp >= 0)
    p = jax.nn.softmax(jnp.where(mask, s, NEG), axis=-1)
    return jnp.einsum('bgqts,bsgd->btgqd', p.astype(v.dtype), v).reshape(B, Tq, NSA_HEADS, HEAD_DIM)


def sb_attend(q, q_pos, k, v, k_pos):
    z = jnp.einsum('bthd,bshd->bhts', q, k).astype(jnp.float32) * HEAD_DIM ** -0.5
    mask = k_pos[None, :] < q_pos[:, None]
    log_1m = jnp.where(mask, jax.nn.log_sigmoid(-z), 0.0)
    suffix = lax.cumsum(log_1m, axis=3, reverse=True) - log_1m
    a = jnp.where(mask, jnp.exp(jax.nn.log_sigmoid(z) + suffix), 0.0)
    return jnp.einsum('bhts,bshd->bthd', a.astype(v.dtype), v)


def finish(x, gates, o_cmp, o_slc, o_win, o_sb, g_on, g_os, w_out, g_ffn, w_gu, w_down):
    B, T, _ = x.shape
    o_nsa = gates[..., 0:1] * o_cmp + gates[..., 1:2] * o_slc + gates[..., 2:3] * o_win
    mix = jnp.concatenate([rms_norm(o_nsa, g_on), rms_norm(o_sb, g_os)], axis=2).reshape(B, T, MIX_WIDTH)
    h = x + mix @ w_out
    gu = rms_norm(h, g_ffn) @ w_gu
    return h + (jax.nn.silu(gu[..., :D_FF]) * gu[..., D_FF:]) @ w_down


def gather_pages(pool, page_table):
    rows = pool[page_table]
    return rows.reshape((page_table.shape[0], -1) + pool.shape[2:])


def layer_step(xp, xs, c_nsa, c_sb, s_win, page_table, g_attn, w_in, g_q, g_k,
               w_ck1, w_ck2, pe_ck, w_cv1, w_cv2, pe_cv, g_on, g_os, w_out, g_ffn, w_gu, w_down):
    cmp_w = (w_ck1, w_ck2, pe_ck, w_cv1, w_cv2, pe_cv)
    out_w = (g_on, g_os, w_out, g_ffn, w_gu, w_down)
    seq = xp.shape[1]
    pos_p = jnp.arange(seq, dtype=jnp.int32)
    q, gates, nsa_p, win_p, q_sb, sb_p = project(rms_norm(xp, g_attn), pos_p, w_in, g_q, g_k)
    o_cmp, o_slc = nsa_context(q, pos_p, nsa_p, *cmp_w)
    kpad = jnp.pad(win_p, ((0, 0), (WINDOW, 0), (0, 0), (0, 0), (0, 0)))

    def win_block(qq, pp):
        n = WINDOW + qq.shape[1]
        kw = lax.dynamic_slice_in_dim(kpad, pp[0], n, axis=1)
        return window_attend(qq, pp, kw[:, :, 0], kw[:, :, 1], pp[0] - WINDOW + jnp.arange(n, dtype=jnp.int32))

    o_win = query_blocked(win_block, q, pos_p, Q_BLOCK)
    o_sb = query_blocked(lambda qq, pp: sb_attend(qq, pp, sb_p[:, :, 0], sb_p[:, :, 1], pos_p), q_sb, pos_p, Q_BLOCK)
    yp = finish(xp, gates, o_cmp, o_slc, o_win, o_sb, *out_w)
    win_keep_p = win_p[:, seq - min(WINDOW, seq):]
    dseq = xs.shape[1]
    pos_s = PAST_LEN + jnp.arange(dseq, dtype=jnp.int32)
    q_s, gates_s, nsa_s, win_s, q_sb_s, sb_s = project(rms_norm(xs, g_attn), pos_s, w_in, g_q, g_k)
    nsa_all = jnp.concatenate([gather_pages(c_nsa, page_table), nsa_s], axis=1)
    o_cmp_s, o_slc_s = nsa_context(q_s, pos_s, nsa_all, *cmp_w)
    wb = s_win.shape[1]
    win_all = jnp.concatenate([s_win, win_s], axis=1)
    o_win_s = window_attend(q_s, pos_s, win_all[:, :, 0], win_all[:, :, 1],
                            PAST_LEN - wb + jnp.arange(wb + dseq, dtype=jnp.int32))
    sb_all = jnp.concatenate([gather_pages(c_sb, page_table), sb_s], axis=1)
    k_pos_s = jnp.arange(PAST_LEN + dseq, dtype=jnp.int32)
    o_sb_s = query_blocked(lambda qq, pp: sb_attend(qq, pp, sb_all[:, :, 0], sb_all[:, :, 1], k_pos_s),
                           q_sb_s, pos_s, Q_BLOCK)
    ys = finish(xs, gates_s, o_cmp_s, o_slc_s, o_win_s, o_sb_s, *out_w)
    return (yp, ys, nsa_p, nsa_s, sb_p, sb_s, win_keep_p, win_all[:, dseq:])


def setup_inputs(seed: int = 0) -> dict:
    key = jax.random.key(seed)
    ks = jax.random.split(key, 24)
    n_pages = PAST_LEN // PAGE_SIZE
    n_pool = (DEC_BATCH * n_pages * 5) // 4
    wb = min(WINDOW, PAST_LEN)
    f32 = jnp.float32

    def w(k, shape, fan_in):
        return jax.random.normal(k, (DEPTH,) + shape, f32) * fan_in ** -0.5

    def g(k, shape):
        return 1.0 + 0.02 * jax.random.normal(k, (DEPTH,) + shape, f32)

    page_table = jax.random.permutation(ks[5], n_pool)[:DEC_BATCH * n_pages].reshape(DEC_BATCH, n_pages).astype(jnp.int32)
    return {
        'x_prompt': jax.random.normal(ks[0], (BATCH, SEQ, D_MODEL), f32),
        'x_sample': jax.random.normal(ks[1], (DEC_BATCH, DEC_SEQ, D_MODEL), f32),
        'cache_nsa_kv': jax.random.normal(ks[2], (DEPTH, n_pool, PAGE_SIZE, 4, NSA_KV_HEADS, HEAD_DIM), f32),
        'cache_sb_kv': jax.random.normal(ks[3], (DEPTH, n_pool, PAGE_SIZE, 2, SB_HEADS, HEAD_DIM), f32),
        'state_nsa_window': jax.random.normal(ks[4], (DEPTH, DEC_BATCH, wb, 2, NSA_KV_HEADS, HEAD_DIM), f32),
        'page_table': page_table,
        'g_attn': g(ks[6], (D_MODEL,)),
        'w_in': w(ks[7], (D_MODEL, PROJ_W), D_MODEL),
        'g_q': g(ks[8], (HEAD_DIM,)),
        'g_k': g(ks[9], (3, HEAD_DIM)),
        'w_cmp1_k': w(ks[10], (CMP_BLOCK * HEAD_DIM, CMP_HIDDEN), CMP_BLOCK * HEAD_DIM),
        'w_cmp2_k': w(ks[11], (CMP_HIDDEN, HEAD_DIM), CMP_HIDDEN),
        'pe_cmp_k': 0.02 * jax.random.normal(ks[12], (DEPTH, CMP_BLOCK, HEAD_DIM), f32),
        'w_cmp1_v': w(ks[13], (CMP_BLOCK * HEAD_DIM, CMP_HIDDEN), CMP_BLOCK * HEAD_DIM),
        'w_cmp2_v': w(ks[14], (CMP_HIDDEN, HEAD_DIM), CMP_HIDDEN),
        'pe_cmp_v': 0.02 * jax.random.normal(ks[15], (DEPTH, CMP_BLOCK, HEAD_DIM), f32),
        'g_out_nsa': g(ks[16], (NSA_HEADS, HEAD_DIM)),
        'g_out_sb': g(ks[17], (SB_HEADS, HEAD_DIM)),
        'w_out': w(ks[18], (MIX_WIDTH, D_MODEL), MIX_WIDTH),
        'g_ffn': g(ks[19], (D_MODEL,)),
        'w_gate_up': w(ks[20], (D_MODEL, 2 * D_FF), D_MODEL),
        'w_down': w(ks[21], (D_FF, D_MODEL), D_FF),
    }


def reference(x_prompt, x_sample, cache_nsa_kv, cache_sb_kv, state_nsa_window, page_table,
              g_attn, w_in, g_q, g_k, w_cmp1_k, w_cmp2_k, pe_cmp_k, w_cmp1_v, w_cmp2_v, pe_cmp_v,
              g_out_nsa, g_out_sb, w_out, g_ffn, w_gate_up, w_down):
    hp, hs = x_prompt, x_sample
    nsa_p_l, nsa_s_l, sb_p_l, sb_s_l, win_p_l, win_s_l = [], [], [], [], [], []
    for l in range(DEPTH):
        hp, hs, nsa_p, nsa_s, sb_p, sb_s, win_p, win_s = layer_step(
            hp, hs, cache_nsa_kv[l], cache_sb_kv[l], state_nsa_window[l], page_table,
            g_attn[l], w_in[l], g_q[l], g_k[l], w_cmp1_k[l], w_cmp2_k[l], pe_cmp_k[l],
            w_cmp1_v[l], w_cmp2_v[l], pe_cmp_v[l], g_out_nsa[l], g_out_sb[l], w_out[l],
            g_ffn[l], w_gate_up[l], w_down[l])
        nsa_p_l.append(nsa_p)
        nsa_s_l.append(nsa_s)
        sb_p_l.append(sb_p)
        sb_s_l.append(sb_s)
        win_p_l.append(win_p)
        win_s_l.append(win_s)
    return (hp, hs, jnp.stack(nsa_p_l), jnp.stack(nsa_s_l), jnp.stack(sb_p_l), jnp.stack(sb_s_l),
            jnp.stack(win_p_l), jnp.stack(win_s_l))
```

```python
import functools
import math

import numpy as np
import jax
import jax.numpy as jnp
from jax import lax
from jax.experimental import pallas as pl
from jax.experimental.pallas import tpu as pltpu

F32 = jnp.float32
BF16 = jnp.bfloat16

D_MODEL = 1024
HEAD_DIM = 64
NSA_HEADS = 8
SB_HEADS = 8
NSA_KV_HEADS = 2
NSA_GROUP = NSA_HEADS // NSA_KV_HEADS
CMP_BLOCK = 32
CMP_STRIDE = 16
CMP_HIDDEN = 128
SEL_BLOCK = 64
N_SEL = 16
N_LOCAL_FORCED = 2
WINDOW = 512
D_FF = 2816
ROPE_THETA = 10000.0
EPS = 1e-6
NEG = -1e30
SCALE = HEAD_DIM ** -0.5

Q_NSA_W = NSA_HEADS * HEAD_DIM
KV_NSA_W = 3 * 2 * NSA_KV_HEADS * HEAD_DIM
GATE_W = 3 * NSA_HEADS
SB_W = 3 * SB_HEADS * HEAD_DIM
OFF_KV = Q_NSA_W
OFF_GATE = OFF_KV + KV_NSA_W
OFF_SB = OFF_GATE + GATE_W

LANES = 128
SUBLANES = 8
VMEM_LIMIT = 56 * 1024 * 1024

P_Q = 0
P_KV = P_Q + Q_NSA_W
P_SB = P_KV + KV_NSA_W
P_GATE = P_SB + SB_W
P_W = P_GATE + LANES


def _dot(a, b):
    return jnp.dot(a, b, preferred_element_type=F32)


def _dot_nt(a, b):
    return lax.dot_general(a, b, (((1,), (1,)), ((), ())), preferred_element_type=F32)


def _split_dot(x, w, passes=2):
    acc = None
    rem = x
    for _ in range(passes):
        piece = rem.astype(BF16)
        term = _dot(piece, w)
        acc = term if acc is None else acc + term
        rem = rem - piece.astype(F32)
    return acc


def _lane_iota(shape):
    return lax.broadcasted_iota(jnp.int32, shape, len(shape) - 1)


def _head_norm_rope(s, gain, cos, sin, bd):
    ms = _dot((s * s).astype(BF16), bd)
    y = s * lax.rsqrt(ms + EPS) * gain
    lane = _lane_iota(y.shape)
    first_half = (lane % HEAD_DIM) < (HEAD_DIM // 2)
    swapped = jnp.where(first_half, pltpu.roll(y, LANES - HEAD_DIM // 2, 1),
                        pltpu.roll(y, HEAD_DIM // 2, 1))
    return y * cos + swapped * sin


def _dup_heads(x):
    lane = _lane_iota(x.shape)
    sw = pltpu.roll(x, HEAD_DIM, 1)
    low = lane < HEAD_DIM
    return jnp.concatenate([jnp.where(low, x, sw), jnp.where(low, sw, x)], axis=1)


def _proj_kernel(x_ref, gattn_ref, w_ref, cos_ref, sin_ref, gains_ref, bd_ref,
                 q_ref, gates_ref, nsa_ref, win_ref, qsb_ref, sbkv_ref,
                 ksd_ref, vsd_ref, kwd_ref, vwd_ref, ksb_ref, vsb_ref):
    x = x_ref[...]
    xn = x * lax.rsqrt(jnp.mean(x * x, axis=-1, keepdims=True) + EPS) * gattn_ref[...]
    xb = xn.astype(BF16)
    cos = cos_ref[...]
    sin = sin_ref[...]
    bd = bd_ref[...]
    gains = gains_ref[...]

    pq = _dot(xb, w_ref[:, P_Q:P_KV])
    for c in range(Q_NSA_W // LANES):
        y = _head_norm_rope(pq[:, c * LANES:(c + 1) * LANES], gains[0:1], cos, sin, bd)
        q_ref[:, c * LANES:(c + 1) * LANES] = (y * SCALE).astype(BF16)

    pkv = _dot(xb, w_ref[:, P_KV:P_SB])
    for br in range(3):
        k = _head_norm_rope(pkv[:, 2 * br * LANES:(2 * br + 1) * LANES], gains[1 + br:2 + br], cos, sin, bd)
        v = pkv[:, (2 * br + 1) * LANES:(2 * br + 2) * LANES]
        if br < 2:
            nsa_ref[:, 2 * br * LANES:(2 * br + 1) * LANES] = k
            nsa_ref[:, (2 * br + 1) * LANES:(2 * br + 2) * LANES] = v
        else:
            win_ref[:, 0:LANES] = k
            win_ref[:, LANES:2 * LANES] = v
        if br == 1:
            ksd_ref[...] = _dup_heads(k).astype(BF16)
            vsd_ref[...] = _dup_heads(v).astype(BF16)
        if br == 2:
            kwd_ref[...] = _dup_heads(k).astype(BF16)
            vwd_ref[...] = _dup_heads(v).astype(BF16)

    psb = _dot(xb, w_ref[:, P_SB:P_GATE])
    sbw = SB_HEADS * HEAD_DIM
    qsb_ref[...] = (psb[:, 0:sbw] * SCALE).astype(BF16)
    sbkv_ref[...] = psb[:, sbw:3 * sbw]
    ksb_ref[...] = psb[:, sbw:2 * sbw].astype(BF16)
    vsb_ref[...] = psb[:, 2 * sbw:3 * sbw].astype(BF16)

    gates_ref[...] = jax.nn.sigmoid(_dot(xb, w_ref[:, P_GATE:P_W]))


def _rope_tables(positions):
    half = HEAD_DIM // 2
    inv = np.exp(-math.log(ROPE_THETA) * np.arange(0, HEAD_DIM, 2, dtype=np.float64) / HEAD_DIM)
    ang = np.asarray(positions, np.float64)[:, None] * inv[None, :]
    cos = np.tile(np.cos(ang), (1, 2 * LANES // HEAD_DIM))
    sin = np.tile(np.concatenate([-np.sin(ang), np.sin(ang)], axis=1), (1, LANES // HEAD_DIM))
    assert cos.shape[1] == LANES and sin.shape[1] == LANES and half * 2 == HEAD_DIM
    return jnp.asarray(cos, F32), jnp.asarray(sin, F32)


def _pack_w_in(w_in):
    gate = w_in[:, OFF_GATE:OFF_SB].reshape(D_MODEL, NSA_HEADS, 3).transpose(0, 2, 1).reshape(D_MODEL, GATE_W)
    gate = jnp.pad(gate, ((0, 0), (0, LANES - GATE_W)))
    return jnp.concatenate([w_in[:, :OFF_GATE], w_in[:, OFF_SB:], gate], axis=1).astype(BF16)


def _head_mean_matrix():
    idx = np.arange(LANES) // HEAD_DIM
    return jnp.asarray((idx[:, None] == idx[None, :]) / HEAD_DIM, BF16)


def _project(x2d, tokens_per_seq, positions, tm, g_attn, w_packed, g_q, g_k):
    n = x2d.shape[0]
    cos, sin = _rope_tables(positions)
    if cos.shape[0] == 1:
        cos = jnp.broadcast_to(cos, (tm, LANES))
        sin = jnp.broadcast_to(sin, (tm, LANES))
        tbl_blocks = 1
    else:
        tbl_blocks = tokens_per_seq // tm
    gains = jnp.concatenate([jnp.tile(g_q[None, :], (1, 2)), jnp.tile(g_k, (1, 2)),
                             jnp.zeros((SUBLANES - 4, LANES), F32)], axis=0)
    row = lambda w: pl.BlockSpec((tm, w), lambda i: (i, 0))
    const = lambda a: pl.BlockSpec(a.shape, lambda i: (0, 0))
    outs = [(Q_NSA_W, BF16), (LANES, F32), (4 * LANES, F32), (2 * LANES, F32), (512, BF16), (1024, F32),
            (256, BF16), (256, BF16), (256, BF16), (256, BF16), (512, BF16), (512, BF16)]
    bd = _head_mean_matrix()
    g2 = g_attn.reshape(1, D_MODEL)
    return pl.pallas_call(
        _proj_kernel,
        grid=(n // tm,),
        in_specs=[row(D_MODEL), const(g2), const(w_packed),
                  pl.BlockSpec((tm, LANES), lambda i: (i % tbl_blocks, 0)),
                  pl.BlockSpec((tm, LANES), lambda i: (i % tbl_blocks, 0)),
                  const(gains), const(bd)],
        out_specs=[row(w) for w, _ in outs],
        out_shape=[jax.ShapeDtypeStruct((n, w), dt) for w, dt in outs],
        compiler_params=pltpu.CompilerParams(dimension_semantics=("parallel",), vmem_limit_bytes=VMEM_LIMIT),
        name="in_proj",
    )(x2d, g2, w_packed, cos, sin, gains, bd)


CHUNK = CMP_STRIDE


def _compress_rows(rows_ref, n_chunks, kv, pe_ref, w1_ref, w2_ref):
    acc_a = jnp.zeros((n_chunks, 2 * CMP_HIDDEN), F32)
    acc_b = jnp.zeros((n_chunks, 2 * CMP_HIDDEN), F32)
    for r in range(CHUNK):
        xr = rows_ref[pl.ds(r, n_chunks, stride=CHUNK), :]
        acc_a += _dot((xr + pe_ref[kv, r:r + 1, :]).astype(BF16), w1_ref[kv, r])
        acc_b += _dot((xr + pe_ref[kv, CHUNK + r:CHUNK + r + 1, :]).astype(BF16), w1_ref[kv, CHUNK + r])
    h = acc_a + pltpu.roll(acc_b, n_chunks - 1, 0)
    return _dot(jax.nn.gelu(h, approximate=True).astype(BF16), w2_ref[kv])


def _cmp_prompt_kernel(krows_ref, vrows_ref, pe_ref, w1_ref, w2_ref, kc_ref, vc_ref):
    n_chunks = krows_ref.shape[0] // CHUNK
    kc_ref[...] = _compress_rows(krows_ref, n_chunks, 0, pe_ref, w1_ref, w2_ref).astype(BF16)
    vc_ref[...] = _compress_rows(vrows_ref, n_chunks, 1, pe_ref, w1_ref, w2_ref).astype(BF16)


def _pack_cmp_weights(w1k, w2k, pek, w1v, w2v, pev):
    eye = jnp.eye(NSA_KV_HEADS, dtype=F32)

    def w1_pack(w1):
        w1r = w1.reshape(CMP_BLOCK, HEAD_DIM, CMP_HIDDEN)
        return jnp.einsum('rdj,gh->rgdhj', w1r, eye).reshape(CMP_BLOCK, LANES, 2 * CMP_HIDDEN)

    def w2_pack(w2):
        return jnp.einsum('jd,gh,u->gjhud', w2, eye, jnp.ones((2,), F32)).reshape(2 * CMP_HIDDEN, 2 * LANES)

    w1 = jnp.stack([w1_pack(w1k), w1_pack(w1v)]).astype(BF16)
    w2 = jnp.stack([w2_pack(w2k), w2_pack(w2v)]).astype(BF16)
    pe = jnp.stack([jnp.tile(pek, (1, 2)), jnp.tile(pev, (1, 2))])
    return pe, w1, w2


def _compress_prompt(nsa_rows, rows_per_step, pe, w1, w2):
    n = nsa_rows.shape[0]
    const = lambda a: pl.BlockSpec(a.shape, lambda i: (0,) * a.ndim)
    cps = rows_per_step // CHUNK
    return pl.pallas_call(
        _cmp_prompt_kernel,
        grid=(n // rows_per_step,),
        in_specs=[pl.BlockSpec((rows_per_step, LANES), lambda i: (i, 0)),
                  pl.BlockSpec((rows_per_step, LANES), lambda i: (i, 1)), const(pe), const(w1), const(w2)],
        out_specs=[pl.BlockSpec((cps, 2 * LANES), lambda i: (i, 0))] * 2,
        out_shape=[jax.ShapeDtypeStruct((n // CHUNK, 2 * LANES), BF16)] * 2,
        compiler_params=pltpu.CompilerParams(dimension_semantics=("parallel",), vmem_limit_bytes=VMEM_LIMIT),
        name="cmp_prompt",
    )(nsa_rows, nsa_rows, pe, w1, w2)


TQ = 128
KCH = 512


def _overlap_matrix(n_valid_cmp, n_rows, n_blocks, n_cols):
    c = np.arange(n_rows)[:, None] * CMP_STRIDE
    j = np.arange(n_cols)[None, :] * SEL_BLOCK
    ov = (c < j + SEL_BLOCK) & (c + CMP_BLOCK > j)
    ov &= (np.arange(n_rows)[:, None] < n_valid_cmp) & (np.arange(n_cols)[None, :] < n_blocks)
    return jnp.asarray(ov, BF16)


def _expand_matrix(n_rows, n_keys):
    j = np.arange(n_rows)[:, None]
    k = np.arange(n_keys)[None, :] // SEL_BLOCK
    return jnp.asarray(j == k, BF16)


def _half_masks(shape):
    lane = _lane_iota(shape)
    return [lane < HEAD_DIM, lane >= HEAD_DIM]


def _topk_mask_t(score_t, n_blocks, k):
    jrow = lax.broadcasted_iota(jnp.int32, score_t.shape, 0)
    cnt = jnp.zeros(score_t.shape, F32)
    for i in range(n_blocks):
        ri = score_t[i:i + 1, :]
        beats = (ri > score_t) | ((ri == score_t) & (i < jrow))
        cnt += jnp.where(beats, 1.0, 0.0)
    return jnp.where((cnt < k) & (score_t > -jnp.inf), 1.0, 0.0)


def _nsa_prompt_kernel(q_ref, kc_ref, vc_ref, ks_ref, vs_ref, ov_ref, ex_ref,
                       ocmp_ref, oslc_ref, selm_ref):
    t0 = pl.program_id(1) * TQ
    n_keys = ks_ref.shape[0]
    n_blocks = n_keys // SEL_BLOCK
    n_cmp = kc_ref.shape[0]
    t_c = t0 + lax.broadcasted_iota(jnp.int32, (TQ, n_cmp), 0)
    cmask = (_lane_iota((TQ, n_cmp)) * CMP_STRIDE + (CMP_BLOCK - 1)) <= t_c
    halves = _half_masks((TQ, LANES))
    t_k = t0 + lax.broadcasted_iota(jnp.int32, (TQ, KCH), 0)
    n_active = (t0 + TQ + KCH - 1) // KCH

    for g in range(NSA_KV_HEADS):
        kc = kc_ref[:, g * LANES:(g + 1) * LANES]
        vc = vc_ref[:, g * LANES:(g + 1) * LANES]
        psum = jnp.zeros((TQ, n_cmp), F32)
        qms = []
        for jp in range(2):
            lo = 2 * g * LANES + jp * LANES
            q128 = q_ref[:, lo:lo + LANES]
            pair = jnp.zeros((TQ, LANES), F32)
            for hh in range(2):
                qm = jnp.where(halves[hh], q128, jnp.zeros_like(q128))
                qms.append(qm)
                s = jnp.where(cmask, _dot_nt(qm, kc), NEG)
                e = jnp.exp(s - jnp.max(s, axis=-1, keepdims=True))
                p = jnp.where(cmask, e / jnp.sum(e, axis=-1, keepdims=True), 0.0)
                psum += p
                pair = jnp.where(halves[hh], _dot(p.astype(BF16), vc), pair)
            ocmp_ref[:, lo:lo + LANES] = pair

        imp = _split_dot(psum, ov_ref[...], passes=3)
        jidx = _lane_iota(imp.shape)
        qblk = (t0 + lax.broadcasted_iota(jnp.int32, imp.shape, 0)) // SEL_BLOCK
        score = jnp.where(jidx > qblk, -jnp.inf,
                          jnp.where((jidx == 0) | (jidx > qblk - N_LOCAL_FORCED), jnp.inf, imp))
        sel_t = _topk_mask_t(score.T[0:n_blocks], n_blocks, N_SEL)
        sel_t = jnp.concatenate([sel_t, jnp.zeros((imp.shape[1] - n_blocks, TQ), F32)], axis=0)
        sel = sel_t.T.astype(BF16)
        for c in range(n_keys // KCH):
            selm_ref[c] = _dot(sel, ex_ref[:, c * KCH:(c + 1) * KCH])

        def body(c, carry):
            k0 = pl.multiple_of(c * KCH, KCH)
            kk = ks_ref[pl.ds(k0, KCH), g * LANES:(g + 1) * LANES]
            vv = vs_ref[pl.ds(k0, KCH), g * LANES:(g + 1) * LANES]
            valid = (selm_ref[c] > 0.5) & ((k0 + _lane_iota((TQ, KCH))) <= t_k)
            out = []
            for h in range(NSA_GROUP):
                m, l, acc = carry[h]
                s = jnp.where(valid, _dot_nt(qms[h], kk), NEG)
                m_new = jnp.maximum(m, jnp.max(s, axis=-1, keepdims=True))
                alpha = jnp.exp(m - m_new)
                e = jnp.exp(s - m_new)
                l = alpha * l + jnp.sum(e, axis=-1, keepdims=True)
                acc = alpha * acc + _dot(e.astype(BF16), vv)
                out.append((m_new, l, acc))
            return tuple(out)

        init = tuple((jnp.full((TQ, 1), NEG, F32), jnp.zeros((TQ, 1), F32), jnp.zeros((TQ, LANES), F32))
                     for _ in range(NSA_GROUP))
        res = lax.fori_loop(0, n_active, body, init)
        for jp in range(2):
            lo = 2 * g * LANES + jp * LANES
            o0 = res[2 * jp][2] / res[2 * jp][1]
            o1 = res[2 * jp + 1][2] / res[2 * jp + 1][1]
            oslc_ref[:, lo:lo + LANES] = jnp.where(halves[0], o0, o1)


def _nsa_prompt(q, kc, vc, ksd, vsd, batch, seq):
    n_cmp = seq // CHUNK
    n_valid = (seq - CMP_BLOCK) // CMP_STRIDE + 1
    n_blocks = seq // SEL_BLOCK
    ov = _overlap_matrix(n_valid, n_cmp, n_blocks, LANES)
    ex = _expand_matrix(LANES, seq)
    nq = seq // TQ
    per_b = lambda rows, w: pl.BlockSpec((rows, w), lambda b, i: (b, 0))
    qspec = pl.BlockSpec((TQ, Q_NSA_W), lambda b, i: (b * nq + i, 0))
    const = lambda a: pl.BlockSpec(a.shape, lambda b, i: (0, 0))
    return pl.pallas_call(
        _nsa_prompt_kernel,
        grid=(batch, nq),
        in_specs=[qspec, per_b(n_cmp, 2 * LANES), per_b(n_cmp, 2 * LANES),
                  per_b(seq, 2 * LANES), per_b(seq, 2 * LANES), const(ov), const(ex)],
        out_specs=[qspec, qspec],
        out_shape=[jax.ShapeDtypeStruct((batch * seq, Q_NSA_W), F32)] * 2,
        scratch_shapes=[pltpu.VMEM((seq // KCH, TQ, KCH), F32)],
        compiler_params=pltpu.CompilerParams(dimension_semantics=("parallel", "arbitrary"),
                                             vmem_limit_bytes=VMEM_LIMIT),
        name="nsa_prompt",
    )(q, kc, vc, ksd, vsd, ov, ex)


WSPAN = WINDOW + TQ


def _win_prompt_kernel(q_ref, k_ref, v_ref, o_ref):
    t0 = pl.program_id(1) * TQ
    seq = k_ref.shape[0]
    start = pl.multiple_of(jnp.clip(t0 - WINDOW, 0, seq - WSPAN), TQ)
    t = t0 + lax.broadcasted_iota(jnp.int32, (TQ, WSPAN), 0)
    kpos = start + _lane_iota((TQ, WSPAN))
    valid = (kpos <= t) & (kpos >= t - WINDOW)
    halves = _half_masks((TQ, LANES))
    for g in range(NSA_KV_HEADS):
        kk = k_ref[pl.ds(start, WSPAN), g * LANES:(g + 1) * LANES]
        vv = v_ref[pl.ds(start, WSPAN), g * LANES:(g + 1) * LANES]
        for jp in range(2):
            lo = 2 * g * LANES + jp * LANES
            q128 = q_ref[:, lo:lo + LANES]
            pair = jnp.zeros((TQ, LANES), F32)
            for hh in range(2):
                qm = jnp.where(halves[hh], q128, jnp.zeros_like(q128))
                s = jnp.where(valid, _dot_nt(qm, kk), NEG)
                e = jnp.exp(s - jnp.max(s, axis=-1, keepdims=True))
                p = e / jnp.sum(e, axis=-1, keepdims=True)
                pair = jnp.where(halves[hh], _dot(p.astype(BF16), vv), pair)
            o_ref[:, lo:lo + LANES] = pair


def _win_prompt(q, kwd, vwd, batch, seq):
    nq = seq // TQ
    per_b = pl.BlockSpec((seq, 2 * LANES), lambda b, i: (b, 0))
    qspec = pl.BlockSpec((TQ, Q_NSA_W), lambda b, i: (b * nq + i, 0))
    return pl.pallas_call(
        _win_prompt_kernel,
        grid=(batch, nq),
        in_specs=[qspec, per_b, per_b],
        out_specs=qspec,
        out_shape=jax.ShapeDtypeStruct((batch * seq, Q_NSA_W), F32),
        compiler_params=pltpu.CompilerParams(dimension_semantics=("parallel", "arbitrary"),
                                             vmem_limit_bytes=VMEM_LIMIT),
        name="win_prompt",
    )(q, kwd, vwd)


def _suffix_matrix(n):
    j = np.arange(n)[:, None]
    s = np.arange(n)[None, :]
    return jnp.asarray(np.concatenate([j >= s, np.ones((n, n), bool)], axis=1), BF16)


def _sb_terms(z):
    lp = jnp.log1p(jnp.exp(-jnp.abs(z)))
    return -jnp.maximum(z, 0.0) - lp, jnp.minimum(z, 0.0) - lp


def _sb_prompt_kernel(q_ref, k_ref, v_ref, u_ref, o_ref):
    i = pl.program_id(2)
    t = i * TQ + lax.broadcasted_iota(jnp.int32, (TQ, TQ), 0)
    halves = _half_masks((TQ, LANES))
    q128 = q_ref[...]
    qms = [jnp.where(h, q128, jnp.zeros_like(q128)) for h in halves]
    u = u_ref[...]

    def body(n, carry):
        k0 = pl.multiple_of((i - n) * TQ, TQ)
        kk = k_ref[pl.ds(k0, TQ), :]
        vv = v_ref[pl.ds(k0, TQ), :]
        mask = (k0 + _lane_iota((TQ, TQ))) < t
        out = []
        for hh in range(2):
            c, acc = carry[hh]
            l1m, lsg = _sb_terms(_dot_nt(qms[hh], kk))
            l1m = jnp.where(mask, l1m, 0.0)
            res = _split_dot(l1m, u, passes=2)
            inc = res[:, :TQ]
            a = jnp.where(mask, jnp.exp(lsg + (inc - l1m) + c), 0.0)
            out.append((c + res[:, TQ:], acc + _dot(a.astype(BF16), vv)))
        return tuple(out)

    zero = jnp.zeros((TQ, LANES), F32)
    res = lax.fori_loop(0, i + 1, body, ((zero, zero), (zero, zero)))
    o_ref[...] = jnp.where(halves[0], res[0][1], res[1][1])


def _sb_prompt(qsb, ksb, vsb, batch, seq):
    nq = seq // TQ
    npair = SB_HEADS // 2
    u = _suffix_matrix(TQ)
    per_b = pl.BlockSpec((seq, LANES), lambda b, j, i: (b, j))
    qspec = pl.BlockSpec((TQ, LANES), lambda b, j, i: (b * nq + i, j))
    return pl.pallas_call(
        _sb_prompt_kernel,
        grid=(batch, npair, nq),
        in_specs=[qspec, per_b, per_b, pl.BlockSpec(u.shape, lambda b, j, i: (0, 0))],
        out_specs=qspec,
        out_shape=jax.ShapeDtypeStruct((batch * seq, SB_HEADS * HEAD_DIM), F32),
        compiler_params=pltpu.CompilerParams(dimension_semantics=("parallel", "parallel", "arbitrary"),
                                             vmem_limit_bytes=VMEM_LIMIT),
        name="sb_prompt",
    )(qsb, ksb, vsb, u)


N_FF_CHUNKS = 2
FF_CHUNK = D_FF // N_FF_CHUNKS


def _gate_expand_matrices():
    m = np.zeros((3, LANES, Q_NSA_W), np.float32)
    for b in range(3):
        for h in range(NSA_HEADS):
            m[b, b * NSA_HEADS + h, h * HEAD_DIM:(h + 1) * HEAD_DIM] = 1.0
    return jnp.asarray(m, BF16)


def _head_rms(o, gain, bd):
    cols = []
    for c in range(o.shape[1] // LANES):
        oc = o[:, c * LANES:(c + 1) * LANES]
        ms = _split_dot(oc * oc, bd, passes=2)
        cols.append(oc * lax.rsqrt(ms + EPS))
    return jnp.concatenate(cols, axis=1) * gain


def _finish_kernel(x_ref, gates_ref, ocmp_ref, oslc_ref, owin_ref, osb_ref, gexp_ref, gon_ref, gos_ref,
                   bd_ref, wout_ref, gffn_ref, wg_ref, wu_ref, wd_ref, y_ref, hn_ref):
    @pl.when(pl.program_id(1) == 0)
    def _():
        gates = gates_ref[...]
        bd = bd_ref[...]
        o_nsa = (_split_dot(gates, gexp_ref[0]) * ocmp_ref[...]
                 + _split_dot(gates, gexp_ref[1]) * oslc_ref[...]
                 + _split_dot(gates, gexp_ref[2]) * owin_ref[...])
        mix = jnp.concatenate([_head_rms(o_nsa, gon_ref[...], bd), _head_rms(osb_ref[...], gos_ref[...], bd)],
                              axis=1)
        h = x_ref[...] + _dot(mix.astype(BF16), wout_ref[...])
        hn = h * lax.rsqrt(jnp.mean(h * h, axis=-1, keepdims=True) + EPS) * gffn_ref[...]
        hn_ref[...] = hn.astype(BF16)
        y_ref[...] = h

    hn = hn_ref[...]
    gate = _dot(hn, wg_ref[...])
    up = _dot(hn, wu_ref[...])
    act = gate * jax.nn.sigmoid(gate) * up
    y_ref[...] += _dot(act.astype(BF16), wd_ref[...])


def _finish(x2d, gates, o_cmp, o_slc, o_win, o_sb, tm, g_on, g_os, w_out, g_ffn, w_gate, w_up, w_down):
    n = x2d.shape[0]
    gexp = _gate_expand_matrices()
    bd = _head_mean_matrix()
    gon = g_on.reshape(1, Q_NSA_W)
    gos = g_os.reshape(1, SB_HEADS * HEAD_DIM)
    gffn = g_ffn.reshape(1, D_MODEL)
    row = lambda w: pl.BlockSpec((tm, w), lambda i, c: (i, 0))
    const = lambda a: pl.BlockSpec(a.shape, lambda i, c: (0,) * a.ndim)
    return pl.pallas_call(
        _finish_kernel,
        grid=(n // tm, N_FF_CHUNKS),
        in_specs=[row(D_MODEL), row(LANES), row(Q_NSA_W), row(Q_NSA_W), row(Q_NSA_W), row(Q_NSA_W),
                  const(gexp), const(gon), const(gos), const(bd), const(w_out), const(gffn),
                  pl.BlockSpec((D_MODEL, FF_CHUNK), lambda i, c: (0, c)),
                  pl.BlockSpec((D_MODEL, FF_CHUNK), lambda i, c: (0, c)),
                  pl.BlockSpec((FF_CHUNK, D_MODEL), lambda i, c: (c, 0))],
        out_specs=row(D_MODEL),
        out_shape=jax.ShapeDtypeStruct((n, D_MODEL), F32),
        scratch_shapes=[pltpu.VMEM((tm, D_MODEL), BF16)],
        compiler_params=pltpu.CompilerParams(dimension_semantics=("parallel", "arbitrary"),
                                             vmem_limit_bytes=VMEM_LIMIT),
        name="finish",
    )(x2d, gates, o_cmp, o_slc, o_win, o_sb, gexp, gon, gos, bd, w_out, gffn, w_gate, w_up, w_down)


ROWS8 = SUBLANES


def _decode_q(q_row, g):
    shape = (ROWS8, LANES)
    row = lax.broadcasted_iota(jnp.int32, shape, 0)
    lane = _lane_iota(shape)
    lo = 2 * g * LANES
    base = jnp.where(row < 2, jnp.broadcast_to(q_row[:, lo:lo + LANES], shape),
                     jnp.broadcast_to(q_row[:, lo + LANES:lo + 2 * LANES], shape))
    val = jnp.where((row % 2) == g, base, pltpu.roll(base, HEAD_DIM, 1))
    keep = (row < NSA_GROUP) & ((lane >= HEAD_DIM) == (g == 1))
    return jnp.where(keep, val, 0.0).astype(BF16)


def _decode_place(o, g):
    row = lax.broadcasted_iota(jnp.int32, o.shape, 0)
    return jnp.where((row % 2) == g, o, pltpu.roll(o, HEAD_DIM, 1))


def _decode_store(o_ref, val, g):
    low = _lane_iota((1, LANES)) < HEAD_DIM
    for jp in range(2):
        lo = 2 * g * LANES + jp * LANES
        o_ref[0, 0:1, lo:lo + LANES] = jnp.where(low, val[2 * jp:2 * jp + 1], val[2 * jp + 1:2 * jp + 2])


def _decode_attend(qg, k128, v128, valid, knew, vnew):
    s = _dot_nt(qg, k128)
    if valid is not None:
        s = jnp.where(valid, s, NEG)
    kn = jnp.broadcast_to(knew, (LANES, LANES)).astype(BF16)
    vn = jnp.broadcast_to(vnew, (LANES, LANES)).astype(BF16)
    s2 = jnp.where(_lane_iota((ROWS8, LANES)) == 0, _dot_nt(qg, kn), NEG)
    m = jnp.maximum(jnp.max(s, axis=-1, keepdims=True), jnp.max(s2, axis=-1, keepdims=True))
    e = jnp.exp(s - m)
    e2 = jnp.exp(s2 - m)
    l = jnp.sum(e, axis=-1, keepdims=True) + jnp.sum(e2, axis=-1, keepdims=True)
    return (_dot(e.astype(BF16), v128) + _dot(e2.astype(BF16), vn)) / l


N_PAGES_MAX = 64
PAGE = 128


def _cmp_sample_kernel(pt_ref, q_ref, cache_ref, pe_ref, w1_ref, w2_ref, ov_ref,
                       ocmp_ref, imp_ref, kbuf_ref, vbuf_ref, sem):
    b = pl.program_id(0)
    n_pages = kbuf_ref.shape[0] // PAGE
    bufs = (kbuf_ref, vbuf_ref)

    def page_copy(p, kv):
        return pltpu.make_async_copy(cache_ref.at[pt_ref[b, p], :, pl.ds(kv * LANES, LANES)],
                                     bufs[kv].at[pl.ds(p * PAGE, PAGE), :], sem.at[kv])

    for p in range(n_pages):
        for kv in range(2):
            page_copy(p, kv).start()
    for p in range(n_pages):
        for kv in range(2):
            page_copy(p, kv).wait()

    n_chunks = kbuf_ref.shape[0] // CHUNK
    kc = _compress_rows(kbuf_ref, n_chunks, 0, pe_ref, w1_ref, w2_ref).astype(BF16)
    vc = _compress_rows(vbuf_ref, n_chunks, 1, pe_ref, w1_ref, w2_ref).astype(BF16)
    q_row = q_ref[0]
    cmask = _lane_iota((ROWS8, n_chunks)) < (n_chunks - 1)
    ocmp_ref[...] = jnp.zeros_like(ocmp_ref)
    imp_ref[...] = jnp.zeros_like(imp_ref)
    for g in range(NSA_KV_HEADS):
        qg = _decode_q(q_row, g)
        s = jnp.where(cmask, _dot_nt(qg, kc[:, g * LANES:(g + 1) * LANES]), NEG)
        e = jnp.exp(s - jnp.max(s, axis=-1, keepdims=True))
        p = jnp.where(cmask, e / jnp.sum(e, axis=-1, keepdims=True), 0.0)
        o = _dot(p.astype(BF16), vc[:, g * LANES:(g + 1) * LANES])
        _decode_store(ocmp_ref, _decode_place(o, g), g)
        row = lax.broadcasted_iota(jnp.int32, p.shape, 0)
        psum = jnp.sum(jnp.where(row < NSA_GROUP, p, 0.0), axis=0, keepdims=True)
        imp = _split_dot(jnp.broadcast_to(psum, p.shape), ov_ref[...], passes=3)
        imp_ref[0, g:g + 1, :] = imp[0:1]


def _cmp_sample(page_table, q3, cache_nsa, pe, w1, w2, past_len):
    db, n_pages = page_table.shape
    n_chunks = past_len // CHUNK
    n_blocks = past_len // SEL_BLOCK + 1
    ncol = 2 * LANES
    assert n_blocks <= ncol
    ov = _overlap_matrix(n_chunks - 1, n_chunks, n_blocks, ncol)
    const = lambda a: pl.BlockSpec(a.shape, lambda b, pt: (0,) * a.ndim)
    gs = pltpu.PrefetchScalarGridSpec(
        num_scalar_prefetch=1,
        grid=(db,),
        in_specs=[pl.BlockSpec((1, 1, Q_NSA_W), lambda b, pt: (b, 0, 0)),
                  pl.BlockSpec(memory_space=pl.ANY), const(pe), const(w1), const(w2), const(ov)],
        out_specs=[pl.BlockSpec((1, ROWS8, Q_NSA_W), lambda b, pt: (b, 0, 0)),
                   pl.BlockSpec((1, ROWS8, ncol), lambda b, pt: (b, 0, 0))],
        scratch_shapes=[pltpu.VMEM((past_len, LANES), F32), pltpu.VMEM((past_len, LANES), F32),
                        pltpu.SemaphoreType.DMA((2,))],
    )
    return pl.pallas_call(
        _cmp_sample_kernel,
        grid_spec=gs,
        out_shape=[jax.ShapeDtypeStruct((db, ROWS8, Q_NSA_W), F32), jax.ShapeDtypeStruct((db, ROWS8, ncol), F32)],
        compiler_params=pltpu.CompilerParams(dimension_semantics=("arbitrary",), vmem_limit_bytes=VMEM_LIMIT),
        name="cmp_sample",
    )(page_table, q3, cache_nsa, pe, w1, w2, ov)


def _topk_sample_kernel(imp_ref, idx_ref, *, n_blocks, q_block):
    imp = imp_ref[...]
    jidx = _lane_iota(imp.shape)
    score = jnp.where(jidx > q_block, -jnp.inf,
                      jnp.where((jidx == 0) | (jidx > q_block - N_LOCAL_FORCED), jnp.inf, imp))
    nb_pad = -(-n_blocks // SUBLANES) * SUBLANES
    st = score.T[0:nb_pad]
    jrow = lax.broadcasted_iota(jnp.int32, st.shape, 0)
    st = jnp.where(jrow < n_blocks, st, -jnp.inf)
    cnt = jnp.zeros(st.shape, F32)
    for i in range(n_blocks):
        ri = st[i:i + 1, :]
        cnt += jnp.where((ri > st) | ((ri == st) & (i < jrow)), 1.0, 0.0)
    jf = jrow.astype(F32)
    rows = []
    for k in range(N_SEL):
        hit = (cnt == k) & (st > -jnp.inf)
        rows.append(jnp.sum(jnp.where(hit, jf, 0.0), axis=0, keepdims=True)
                    + jnp.where(jnp.sum(jnp.where(hit, 1.0, 0.0), axis=0, keepdims=True) > 0.5, 0.0, -1.0))
    idx_ref[...] = jnp.concatenate(rows, axis=0).astype(jnp.int32)


def _topk_sample(imp2d, n_blocks, q_block):
    rows = imp2d.shape[0]
    return pl.pallas_call(
        functools.partial(_topk_sample_kernel, n_blocks=n_blocks, q_block=q_block),
        out_shape=jax.ShapeDtypeStruct((N_SEL, rows), jnp.int32),
        compiler_params=pltpu.CompilerParams(vmem_limit_bytes=VMEM_LIMIT),
        name="topk_sample",
    )(imp2d)


def _slc_sample_kernel(pt_ref, idx_ref, q_ref, new_ref, cache_ref, o_ref, buf_ref, sem, *, n_past_blocks):
    b = pl.program_id(0)
    per_page = PAGE // SEL_BLOCK

    def block_copy(g, k):
        j = jnp.clip(idx_ref[(b * NSA_KV_HEADS + g) * N_SEL + k], 0, n_past_blocks - 1)
        page = pt_ref[b, j // per_page]
        r0 = pl.multiple_of((j % per_page) * SEL_BLOCK, SEL_BLOCK)
        return pltpu.make_async_copy(cache_ref.at[page, pl.ds(r0, SEL_BLOCK), pl.ds(2 * LANES, 2 * LANES)],
                                     buf_ref.at[g, pl.ds(k * SEL_BLOCK, SEL_BLOCK), :], sem)

    for g in range(NSA_KV_HEADS):
        for k in range(N_SEL):
            block_copy(g, k).start()
    for g in range(NSA_KV_HEADS):
        for k in range(N_SEL):
            block_copy(g, k).wait()

    q_row = q_ref[0]
    new = new_ref[0]
    o_ref[...] = jnp.zeros_like(o_ref)
    n_keys = N_SEL * SEL_BLOCK
    slot = _lane_iota((ROWS8, n_keys)) // SEL_BLOCK
    for g in range(NSA_KV_HEADS):
        valid = jnp.zeros((ROWS8, n_keys), jnp.bool_)
        for k in range(N_SEL):
            j = idx_ref[(b * NSA_KV_HEADS + g) * N_SEL + k]
            valid = valid | ((slot == k) & ((j >= 0) & (j < n_past_blocks)))
        o = _decode_attend(_decode_q(q_row, g), buf_ref[g, :, 0:LANES].astype(BF16),
                           buf_ref[g, :, LANES:2 * LANES].astype(BF16), valid,
                           new[:, 2 * LANES:3 * LANES], new[:, 3 * LANES:4 * LANES])
        _decode_store(o_ref, _decode_place(o, g), g)


def _slc_sample(page_table, idx_flat, q3, nsa_new3, cache_nsa, past_len):
    db = page_table.shape[0]
    gs = pltpu.PrefetchScalarGridSpec(
        num_scalar_prefetch=2,
        grid=(db,),
        in_specs=[pl.BlockSpec((1, 1, Q_NSA_W), lambda b, pt, ix: (b, 0, 0)),
                  pl.BlockSpec((1, 1, 4 * LANES), lambda b, pt, ix: (b, 0, 0)),
                  pl.BlockSpec(memory_space=pl.ANY)],
        out_specs=pl.BlockSpec((1, ROWS8, Q_NSA_W), lambda b, pt, ix: (b, 0, 0)),
        scratch_shapes=[pltpu.VMEM((NSA_KV_HEADS, N_SEL * SEL_BLOCK, 2 * LANES), F32),
                        pltpu.SemaphoreType.DMA(())],
    )
    return pl.pallas_call(
        functools.partial(_slc_sample_kernel, n_past_blocks=past_len // SEL_BLOCK),
        grid_spec=gs,
        out_shape=jax.ShapeDtypeStruct((db, ROWS8, Q_NSA_W), F32),
        compiler_params=pltpu.CompilerParams(dimension_semantics=("arbitrary",), vmem_limit_bytes=VMEM_LIMIT),
        name="slc_sample",
    )(page_table, idx_flat, q3, nsa_new3, cache_nsa)


def _win_sample_kernel(q_ref, new_ref, win_ref, o_ref):
    q_row = q_ref[0]
    new = new_ref[0]
    o_ref[...] = jnp.zeros_like(o_ref)
    for g in range(NSA_KV_HEADS):
        o = _decode_attend(_decode_q(q_row, g), win_ref[0, :, 0:LANES].astype(BF16),
                           win_ref[0, :, LANES:2 * LANES].astype(BF16), None,
                           new[:, 0:LANES], new[:, LANES:2 * LANES])
        _decode_store(o_ref, _decode_place(o, g), g)


def _win_sample(q3, win_new3, state_win):
    db, wb, w = state_win.shape
    return pl.pallas_call(
        _win_sample_kernel,
        grid=(db,),
        in_specs=[pl.BlockSpec((1, 1, Q_NSA_W), lambda b: (b, 0, 0)),
                  pl.BlockSpec((1, 1, w), lambda b: (b, 0, 0)),
                  pl.BlockSpec((1, wb, w), lambda b: (b, 0, 0))],
        out_specs=pl.BlockSpec((1, ROWS8, Q_NSA_W), lambda b: (b, 0, 0)),
        out_shape=jax.ShapeDtypeStruct((db, ROWS8, Q_NSA_W), F32),
        compiler_params=pltpu.CompilerParams(dimension_semantics=("parallel",), vmem_limit_bytes=VMEM_LIMIT),
        name="win_sample",
    )(q3, win_new3, state_win)


SB_GROUP_PAGES = 8
SB_GROUP_ROWS = SB_GROUP_PAGES * PAGE


def _sb_sample_kernel(pt_ref, q_ref, cache_ref, u_ref, o_ref, buf_ref, sem):
    b = pl.program_id(0)
    n_groups = pt_ref.shape[1] // SB_GROUP_PAGES
    sbw = SB_HEADS * HEAD_DIM

    def page_copy(grp, slot, p):
        return pltpu.make_async_copy(cache_ref.at[pt_ref[b, grp * SB_GROUP_PAGES + p]],
                                     buf_ref.at[slot, pl.ds(p * PAGE, PAGE), :], sem.at[slot])

    def start_group(grp, slot):
        for p in range(SB_GROUP_PAGES):
            page_copy(grp, slot, p).start()

    def wait_group(grp, slot):
        for p in range(SB_GROUP_PAGES):
            page_copy(grp, slot, p).wait()

    shape = (ROWS8, sbw)
    head_lanes = lax.broadcasted_iota(jnp.int32, shape, 0) == (_lane_iota(shape) // HEAD_DIM)
    qbd = jnp.where(head_lanes, jnp.broadcast_to(q_ref[0], shape), 0.0).astype(BF16)
    u = u_ref[...]
    n_ch = SB_GROUP_ROWS // LANES

    start_group(n_groups - 1, (n_groups - 1) % 2)

    def body(n, carry):
        c, acc = carry
        grp = n_groups - 1 - n
        slot = grp % 2

        @pl.when(grp > 0)
        def _():
            start_group(grp - 1, 1 - slot)

        wait_group(grp, slot)
        kk = buf_ref[slot, :, 0:sbw].astype(BF16)
        vv = buf_ref[slot, :, sbw:2 * sbw].astype(BF16)
        l1m, lsg = _sb_terms(_dot_nt(qbd, kk))
        stacked = jnp.concatenate([l1m[:, i * LANES:(i + 1) * LANES] for i in range(n_ch)], axis=0)
        res = _split_dot(stacked, u, passes=2)
        a_chunks = [None] * n_ch
        for i in reversed(range(n_ch)):
            sl = slice(i * LANES, (i + 1) * LANES)
            inc = res[i * ROWS8:(i + 1) * ROWS8, 0:LANES]
            a_chunks[i] = jnp.exp(lsg[:, sl] + (inc - l1m[:, sl]) + c)
            c = c + res[i * ROWS8:(i + 1) * ROWS8, LANES:2 * LANES]
        a = jnp.concatenate(a_chunks, axis=1).astype(BF16)
        return c, acc + _dot(a, vv)

    c0 = jnp.zeros((ROWS8, LANES), F32)
    _, acc = lax.fori_loop(0, n_groups, body, (c0, jnp.zeros(shape, F32)))
    o_ref[...] = jnp.zeros_like(o_ref)
    o_ref[0, 0:1, :] = jnp.sum(jnp.where(head_lanes, acc, 0.0), axis=0, keepdims=True)


def _sb_sample(page_table, qsb3, cache_sb):
    db = page_table.shape[0]
    w = cache_sb.shape[-1]
    sbw = SB_HEADS * HEAD_DIM
    u = _suffix_matrix(LANES)
    gs = pltpu.PrefetchScalarGridSpec(
        num_scalar_prefetch=1,
        grid=(db,),
        in_specs=[pl.BlockSpec((1, 1, sbw), lambda b, pt: (b, 0, 0)),
                  pl.BlockSpec(memory_space=pl.ANY),
                  pl.BlockSpec(u.shape, lambda b, pt: (0, 0))],
        out_specs=pl.BlockSpec((1, ROWS8, sbw), lambda b, pt: (b, 0, 0)),
        scratch_shapes=[pltpu.VMEM((2, SB_GROUP_ROWS, w), F32), pltpu.SemaphoreType.DMA((2,))],
    )
    return pl.pallas_call(
        _sb_sample_kernel,
        grid_spec=gs,
        out_shape=jax.ShapeDtypeStruct((db, ROWS8, sbw), F32),
        compiler_params=pltpu.CompilerParams(dimension_semantics=("arbitrary",), vmem_limit_bytes=VMEM_LIMIT),
        name="sb_sample",
    )(page_table, qsb3, cache_sb, u)


PROMPT_TM = 256


def kernel(x_prompt, x_sample, cache_nsa_kv, cache_sb_kv, state_nsa_window, page_table, g_attn, w_in, g_q, g_k,
           w_cmp1_k, w_cmp2_k, pe_cmp_k, w_cmp1_v, w_cmp2_v, pe_cmp_v, g_out_nsa, g_out_sb, w_out, g_ffn,
           w_gate_up, w_down):
    depth = w_in.shape[0]
    assert depth == 1
    batch, seq, _ = x_prompt.shape
    db, dseq, _ = x_sample.shape
    assert dseq == 1
    n_pool, page = cache_nsa_kv.shape[1:3]
    past_len = page_table.shape[1] * page
    wb = state_nsa_window.shape[2]
    assert page == PAGE and wb == WINDOW

    l = 0
    w_packed = _pack_w_in(w_in[l])
    pe, w1, w2 = _pack_cmp_weights(w_cmp1_k[l], w_cmp2_k[l], pe_cmp_k[l], w_cmp1_v[l], w_cmp2_v[l], pe_cmp_v[l])
    fin_w = (g_out_nsa[l], g_out_sb[l], w_out[l].astype(BF16), g_ffn[l],
             w_gate_up[l][:, :D_FF].astype(BF16), w_gate_up[l][:, D_FF:].astype(BF16), w_down[l].astype(BF16))

    xp = x_prompt.reshape(batch * seq, D_MODEL)
    (q, gates, nsa, win, qsb, sbkv, ksd, vsd, kwd, vwd, ksb, vsb) = _project(
        xp, seq, np.arange(seq), PROMPT_TM, g_attn[l], w_packed, g_q[l], g_k[l])
    kc, vc = _compress_prompt(nsa, seq, pe, w1, w2)
    o_cmp, o_slc = _nsa_prompt(q, kc, vc, ksd, vsd, batch, seq)
    o_win = _win_prompt(q, kwd, vwd, batch, seq)
    o_sb = _sb_prompt(qsb, ksb, vsb, batch, seq)
    yp = _finish(xp, gates, o_cmp, o_slc, o_win, o_sb, PROMPT_TM, *fin_w)

    xs = x_sample.reshape(db, D_MODEL)
    (q_s, gates_s, nsa_s, win_s, qsb_s, sbkv_s, _, _, _, _, _, _) = _project(
        xs, 1, np.full((1,), past_len), db, g_attn[l], w_packed, g_q[l], g_k[l])
    q3 = q_s.astype(F32).reshape(db, 1, Q_NSA_W)
    cache_nsa = cache_nsa_kv[l].reshape(n_pool, page, 4 * LANES)
    cache_sb = cache_sb_kv[l].reshape(n_pool, page, 2 * SB_HEADS * HEAD_DIM)
    o_cmp_s, imp = _cmp_sample(page_table, q3, cache_nsa, pe, w1, w2, past_len)
    n_blocks = past_len // SEL_BLOCK + 1
    idx = _topk_sample(imp.reshape(db * ROWS8, imp.shape[-1]), n_blocks, past_len // SEL_BLOCK)
    idx = idx.reshape(N_SEL, db, ROWS8)[:, :, :NSA_KV_HEADS].transpose(1, 2, 0).reshape(-1)
    o_slc_s = _slc_sample(page_table, idx, q3, nsa_s.reshape(db, 1, 4 * LANES), cache_nsa, past_len)
    s_win = state_nsa_window[l].reshape(db, wb, 2 * LANES)
    o_win_s = _win_sample(q3, win_s.reshape(db, 1, 2 * LANES), s_win)
    o_sb_s = _sb_sample(page_table, qsb_s.astype(F32).reshape(db, 1, SB_HEADS * HEAD_DIM), cache_sb)
    ys = _finish(xs, gates_s, o_cmp_s[:, 0], o_slc_s[:, 0], o_win_s[:, 0], o_sb_s[:, 0], db, *fin_w)

    win_keep = win.reshape(batch, seq, 2 * LANES)[:, seq - min(WINDOW, seq):]
    win_all = jnp.concatenate([s_win[:, dseq:], win_s.reshape(db, dseq, 2 * LANES)], axis=1)
    kvh = (NSA_KV_HEADS, HEAD_DIM)
    return (yp.reshape(batch, seq, D_MODEL), ys.reshape(db, dseq, D_MODEL),
            nsa.reshape((depth, batch, seq, 4) + kvh), nsa_s.reshape((depth, db, dseq, 4) + kvh),
            sbkv.reshape(depth, batch, seq, 2, SB_HEADS, HEAD_DIM),
            sbkv_s.reshape(depth, db, dseq, 2, SB_HEADS, HEAD_DIM),
            win_keep.reshape((depth, batch, min(WINDOW, seq), 2) + kvh),
            win_all.reshape((depth, db, wb, 2) + kvh))
```

```python
import functools
import math

import numpy as np
import jax
import jax.numpy as jnp
from jax import lax
from jax.experimental import pallas as pl
from jax.experimental.pallas import tpu as pltpu

F32 = jnp.float32
BF16 = jnp.bfloat16

D_MODEL = 1024
HEAD_DIM = 64
NSA_HEADS = 8
SB_HEADS = 8
NSA_KV_HEADS = 2
NSA_GROUP = NSA_HEADS // NSA_KV_HEADS
CMP_BLOCK = 32
CMP_STRIDE = 16
CMP_HIDDEN = 128
SEL_BLOCK = 64
N_SEL = 16
N_LOCAL_FORCED = 2
WINDOW = 512
D_FF = 2816
ROPE_THETA = 10000.0
EPS = 1e-6
NEG = -1e30
SCALE = HEAD_DIM ** -0.5

Q_NSA_W = NSA_HEADS * HEAD_DIM
KV_NSA_W = 3 * 2 * NSA_KV_HEADS * HEAD_DIM
GATE_W = 3 * NSA_HEADS
SB_QW = SB_HEADS * HEAD_DIM
SB_W = 3 * SB_QW
OFF_KV = Q_NSA_W
OFF_GATE = OFF_KV + KV_NSA_W
OFF_SB = OFF_GATE + GATE_W

LANES = 128
SUBLANES = 8
VMEM_LIMIT = 56 * 1024 * 1024

P_Q = 0
P_KV = P_Q + Q_NSA_W
P_SB = P_KV + KV_NSA_W
P_GATE = P_SB + SB_W
P_W = P_GATE + LANES

KT = LANES


def _dot(a, b):
    return jnp.dot(a, b, preferred_element_type=F32)


def _dot_nt(a, b):
    return lax.dot_general(a, b, (((1,), (1,)), ((), ())), preferred_element_type=F32)


def _split_dot(x, w, passes=2):
    acc = None
    rem = x
    for _ in range(passes):
        piece = rem.astype(BF16)
        term = _dot(piece, w)
        acc = term if acc is None else acc + term
        rem = rem - piece.astype(F32)
    return acc


def _lane_iota(shape):
    return lax.broadcasted_iota(jnp.int32, shape, len(shape) - 1)


def _row_iota(shape):
    return lax.broadcasted_iota(jnp.int32, shape, 0)


def _half_masks(shape):
    lane = _lane_iota(shape)
    return [lane < HEAD_DIM, lane >= HEAD_DIM]


def _twice(x):
    return jnp.concatenate([x, x], axis=0)


def _head_norm_rope(s, gain, cos, sin, bd):
    ms = _dot((s * s).astype(BF16), bd)
    y = s * lax.rsqrt(ms + EPS) * gain
    lane = _lane_iota(y.shape)
    first_half = (lane % HEAD_DIM) < (HEAD_DIM // 2)
    swapped = jnp.where(first_half, pltpu.roll(y, LANES - HEAD_DIM // 2, 1),
                        pltpu.roll(y, HEAD_DIM // 2, 1))
    return y * cos + swapped * sin


def _proj_sections(x_ref, gattn_ref, w_ref, cos_ref, sin_ref, gains_ref, bd_ref):
    x = x_ref[...]
    xn = x * lax.rsqrt(jnp.mean(x * x, axis=-1, keepdims=True) + EPS) * gattn_ref[...]
    xb = xn.astype(BF16)
    cos = cos_ref[...]
    sin = sin_ref[...]
    bd = bd_ref[...]
    gains = gains_ref[...]
    pq = _dot(xb, w_ref[:, P_Q:P_KV])
    q = [_head_norm_rope(pq[:, c * LANES:(c + 1) * LANES], gains[0:1], cos, sin, bd) * SCALE
         for c in range(Q_NSA_W // LANES)]
    pkv = _dot(xb, w_ref[:, P_KV:P_SB])
    kv = []
    for br in range(3):
        kv.append(_head_norm_rope(pkv[:, 2 * br * LANES:(2 * br + 1) * LANES], gains[1 + br:2 + br], cos, sin, bd))
        kv.append(pkv[:, (2 * br + 1) * LANES:(2 * br + 2) * LANES])
    psb = _dot(xb, w_ref[:, P_SB:P_GATE])
    gates = jax.nn.sigmoid(_dot(xb, w_ref[:, P_GATE:P_W]))
    return q, kv, psb, gates


def _proj_sample_kernel(x_ref, gattn_ref, w_ref, cos_ref, sin_ref, gains_ref, bd_ref,
                        q_ref, gates_ref, nsa_ref, win_ref, qsb_ref, sbkv_ref):
    q, kv, psb, gates = _proj_sections(x_ref, gattn_ref, w_ref, cos_ref, sin_ref, gains_ref, bd_ref)
    for c in range(4):
        q_ref[:, c * LANES:(c + 1) * LANES] = q[c]
        nsa_ref[:, c * LANES:(c + 1) * LANES] = kv[c]
    for c in range(2):
        win_ref[:, c * LANES:(c + 1) * LANES] = kv[4 + c]
    qsb_ref[...] = psb[:, 0:SB_QW] * SCALE
    sbkv_ref[...] = psb[:, SB_QW:SB_W]
    gates_ref[...] = gates


def _proj_prompt_kernel(x_ref, gattn_ref, w_ref, cos_ref, sin_ref, gains_ref, bd_ref,
                        q_ref, gates_ref, cmp_ref, qsb_ref, nsat_ref, wint_ref, sbkvt_ref,
                        slcb_ref, winb_ref, sbb_ref):
    q, kv, psb, gates = _proj_sections(x_ref, gattn_ref, w_ref, cos_ref, sin_ref, gains_ref, bd_ref)
    tm = x_ref.shape[0]
    for c in range(4):
        q_ref[:, c * LANES:(c + 1) * LANES] = q[c].astype(BF16)
    for c in range(2):
        cmp_ref[:, c * LANES:(c + 1) * LANES] = kv[c]
    qsb_ref[...] = (psb[:, 0:SB_QW] * SCALE).astype(BF16)
    gates_ref[...] = gates

    def store_t(f32_ref, bf_ref, row0, bf_row0, xt):
        f32_ref[0, row0:row0 + LANES, :] = xt
        if bf_ref is not None:
            for u in range(tm // KT):
                bf_ref[0, u, bf_row0:bf_row0 + LANES, :] = xt[:, u * KT:(u + 1) * KT].astype(BF16)

    for c in range(4):
        store_t(nsat_ref, slcb_ref if c >= 2 else None, c * LANES, (c - 2) * LANES, kv[c].T)
    for c in range(2):
        store_t(wint_ref, winb_ref, c * LANES, c * LANES, kv[4 + c].T)
    for c in range(2 * SB_QW // LANES):
        store_t(sbkvt_ref, sbb_ref, c * LANES, c * LANES, psb[:, SB_QW + c * LANES:SB_QW + (c + 1) * LANES].T)


def _rope_tables(positions):
    inv = np.exp(-math.log(ROPE_THETA) * np.arange(0, HEAD_DIM, 2, dtype=np.float64) / HEAD_DIM)
    ang = np.asarray(positions, np.float64)[:, None] * inv[None, :]
    cos = np.tile(np.cos(ang), (1, 2 * LANES // HEAD_DIM))
    sin = np.tile(np.concatenate([-np.sin(ang), np.sin(ang)], axis=1), (1, LANES // HEAD_DIM))
    assert cos.shape[1] == LANES and sin.shape[1] == LANES
    return jnp.asarray(cos, F32), jnp.asarray(sin, F32)


def _pack_w_in(w_in):
    gate = w_in[:, OFF_GATE:OFF_SB].reshape(D_MODEL, NSA_HEADS, 3).transpose(0, 2, 1).reshape(D_MODEL, GATE_W)
    gate = jnp.pad(gate, ((0, 0), (0, LANES - GATE_W)))
    return jnp.concatenate([w_in[:, :OFF_GATE], w_in[:, OFF_SB:], gate], axis=1).astype(BF16)


def _head_mean_matrix():
    idx = np.arange(LANES) // HEAD_DIM
    return jnp.asarray((idx[:, None] == idx[None, :]) / HEAD_DIM, BF16)


def _proj_common(positions, tm, tokens_per_seq, g_attn, g_q, g_k):
    cos, sin = _rope_tables(positions)
    if cos.shape[0] == 1:
        cos = jnp.broadcast_to(cos, (tm, LANES))
        sin = jnp.broadcast_to(sin, (tm, LANES))
        tbl_blocks = 1
    else:
        tbl_blocks = tokens_per_seq // tm
    gains = jnp.concatenate([jnp.tile(g_q[None, :], (1, 2)), jnp.tile(g_k, (1, 2)),
                             jnp.zeros((SUBLANES - 4, LANES), F32)], axis=0)
    return cos, sin, tbl_blocks, gains, _head_mean_matrix(), g_attn.reshape(1, D_MODEL)


def _project_sample(x2d, position, g_attn, w_packed, g_q, g_k):
    n = x2d.shape[0]
    cos, sin, _, gains, bd, g2 = _proj_common(np.full((1,), position), n, 1, g_attn, g_q, g_k)
    full = lambda a: pl.BlockSpec(a.shape, lambda i: (0, 0))
    widths = [Q_NSA_W, LANES, 4 * LANES, 2 * LANES, SB_QW, 2 * SB_QW]
    return pl.pallas_call(
        _proj_sample_kernel,
        grid=(1,),
        in_specs=[full(x2d), full(g2), full(w_packed), full(cos), full(sin), full(gains), full(bd)],
        out_specs=[pl.BlockSpec((n, w), lambda i: (0, 0)) for w in widths],
        out_shape=[jax.ShapeDtypeStruct((n, w), F32) for w in widths],
        compiler_params=pltpu.CompilerParams(dimension_semantics=("arbitrary",), vmem_limit_bytes=VMEM_LIMIT),
        name="in_proj_sample",
    )(x2d, g2, w_packed, cos, sin, gains, bd)


def _project_prompt(x2d, batch, seq, tm, g_attn, w_packed, g_q, g_k):
    cos, sin, tbl_blocks, gains, bd, g2 = _proj_common(np.arange(seq), tm, seq, g_attn, g_q, g_k)
    nt = seq // tm
    row = lambda w: pl.BlockSpec((tm, w), lambda i: (i, 0))
    const = lambda a: pl.BlockSpec(a.shape, lambda i: (0, 0))
    feat = lambda f: pl.BlockSpec((1, f, tm), lambda i: (i // nt, 0, i % nt))
    featb = lambda f: pl.BlockSpec((1, tm // KT, f, KT), lambda i: (i // nt, i % nt, 0, 0))
    n = batch * seq
    sds = jax.ShapeDtypeStruct
    out_shape = [sds((n, Q_NSA_W), BF16), sds((n, LANES), F32), sds((n, 2 * LANES), F32), sds((n, SB_QW), BF16),
                 sds((batch, 4 * LANES, seq), F32), sds((batch, 2 * LANES, seq), F32),
                 sds((batch, 2 * SB_QW, seq), F32),
                 sds((batch, seq // KT, 2 * LANES, KT), BF16), sds((batch, seq // KT, 2 * LANES, KT), BF16),
                 sds((batch, seq // KT, 2 * SB_QW, KT), BF16)]
    out_specs = [row(Q_NSA_W), row(LANES), row(2 * LANES), row(SB_QW),
                 feat(4 * LANES), feat(2 * LANES), feat(2 * SB_QW),
                 featb(2 * LANES), featb(2 * LANES), featb(2 * SB_QW)]
    return pl.pallas_call(
        _proj_prompt_kernel,
        grid=(n // tm,),
        in_specs=[row(D_MODEL), const(g2), const(w_packed),
                  pl.BlockSpec((tm, LANES), lambda i: (i % tbl_blocks, 0)),
                  pl.BlockSpec((tm, LANES), lambda i: (i % tbl_blocks, 0)),
                  const(gains), const(bd)],
        out_specs=out_specs,
        out_shape=out_shape,
        compiler_params=pltpu.CompilerParams(dimension_semantics=("parallel",), vmem_limit_bytes=VMEM_LIMIT),
        name="in_proj_prompt",
    )(x2d, g2, w_packed, cos, sin, gains, bd)


CHUNK = CMP_STRIDE


def _compress_rows(rows_ref, n_chunks, kv, pe_ref, w1_ref, w2_ref):
    acc_a = jnp.zeros((n_chunks, 2 * CMP_HIDDEN), F32)
    acc_b = jnp.zeros((n_chunks, 2 * CMP_HIDDEN), F32)
    for r in range(CHUNK):
        xr = rows_ref[pl.ds(r, n_chunks, stride=CHUNK), :]
        acc_a += _dot((xr + pe_ref[kv, r:r + 1, :]).astype(BF16), w1_ref[kv, r])
        acc_b += _dot((xr + pe_ref[kv, CHUNK + r:CHUNK + r + 1, :]).astype(BF16), w1_ref[kv, CHUNK + r])
    h = acc_a + pltpu.roll(acc_b, n_chunks - 1, 0)
    return _dot(jax.nn.gelu(h, approximate=True).astype(BF16), w2_ref[kv])


def _cmp_prompt_kernel(krows_ref, vrows_ref, pe_ref, w1_ref, w2_ref, kc_ref, vc_ref):
    n_chunks = krows_ref.shape[0] // CHUNK
    kc_ref[...] = _compress_rows(krows_ref, n_chunks, 0, pe_ref, w1_ref, w2_ref).astype(BF16)
    vc_ref[...] = _compress_rows(vrows_ref, n_chunks, 1, pe_ref, w1_ref, w2_ref).astype(BF16)


def _pack_cmp_weights(w1k, w2k, pek, w1v, w2v, pev):
    eye = jnp.eye(NSA_KV_HEADS, dtype=F32)

    def w1_pack(w1):
        w1r = w1.reshape(CMP_BLOCK, HEAD_DIM, CMP_HIDDEN)
        return jnp.einsum('rdj,gh->rgdhj', w1r, eye).reshape(CMP_BLOCK, LANES, 2 * CMP_HIDDEN)

    def w2_pack(w2):
        return jnp.einsum('jd,gh,u->gjhud', w2, eye, jnp.ones((2,), F32)).reshape(2 * CMP_HIDDEN, 2 * LANES)

    w1 = jnp.stack([w1_pack(w1k), w1_pack(w1v)]).astype(BF16)
    w2 = jnp.stack([w2_pack(w2k), w2_pack(w2v)]).astype(BF16)
    pe = jnp.stack([jnp.tile(pek, (1, 2)), jnp.tile(pev, (1, 2))])
    return pe, w1, w2


def _compress_prompt(cmp_rows, rows_per_step, pe, w1, w2):
    n = cmp_rows.shape[0]
    const = lambda a: pl.BlockSpec(a.shape, lambda i: (0,) * a.ndim)
    cps = rows_per_step // CHUNK
    return pl.pallas_call(
        _cmp_prompt_kernel,
        grid=(n // rows_per_step,),
        in_specs=[pl.BlockSpec((rows_per_step, LANES), lambda i: (i, 0)),
                  pl.BlockSpec((rows_per_step, LANES), lambda i: (i, 1)), const(pe), const(w1), const(w2)],
        out_specs=[pl.BlockSpec((cps, 2 * LANES), lambda i: (i, 0))] * 2,
        out_shape=[jax.ShapeDtypeStruct((n // CHUNK, 2 * LANES), BF16)] * 2,
        compiler_params=pltpu.CompilerParams(dimension_semantics=("parallel",), vmem_limit_bytes=VMEM_LIMIT),
        name="cmp_prompt",
    )(cmp_rows, cmp_rows, pe, w1, w2)


TQ = 128
KCH = 512
KCH_CHUNKS = KCH // KT


def _overlap_matrix(n_valid_cmp, n_rows, n_blocks, n_cols):
    c = np.arange(n_rows)[:, None] * CMP_STRIDE
    j = np.arange(n_cols)[None, :] * SEL_BLOCK
    ov = (c < j + SEL_BLOCK) & (c + CMP_BLOCK > j)
    ov &= (np.arange(n_rows)[:, None] < n_valid_cmp) & (np.arange(n_cols)[None, :] < n_blocks)
    return jnp.asarray(ov, BF16)


def _expand_matrix(n_rows, n_keys):
    j = np.arange(n_rows)[:, None]
    k = np.arange(n_keys)[None, :] // SEL_BLOCK
    return jnp.asarray(j == k, BF16)


def _topk_mask_t(score_t, n_blocks, k):
    jrow = _row_iota(score_t.shape)
    cnt = jnp.zeros(score_t.shape, F32)
    for i in range(n_blocks):
        ri = score_t[i:i + 1, :]
        beats = (ri > score_t) | ((ri == score_t) & (i < jrow))
        cnt += jnp.where(beats, 1.0, 0.0)
    return jnp.where((cnt < k) & (score_t > -jnp.inf), 1.0, 0.0)


def _nsa_prompt_kernel(q_ref, kc_ref, vc_ref, slc_ref, ov_ref, ex_ref, ocmp_ref, oslc_ref, selm_ref):
    t0 = pl.program_id(1) * TQ
    n_keys = slc_ref.shape[1] * KT
    n_blocks = n_keys // SEL_BLOCK
    n_cmp = kc_ref.shape[0]
    t_c = t0 + _row_iota((TQ, n_cmp))
    cmask = (_lane_iota((TQ, n_cmp)) * CMP_STRIDE + (CMP_BLOCK - 1)) <= t_c
    halves = _half_masks((TQ, LANES))
    t_k = t0 + _row_iota((TQ, KCH))
    n_active = (t0 + TQ + KCH - 1) // KCH

    for g in range(NSA_KV_HEADS):
        kc = kc_ref[:, g * LANES:(g + 1) * LANES]
        vc = vc_ref[:, g * LANES:(g + 1) * LANES]
        psum = jnp.zeros((TQ, n_cmp), F32)
        qms = []
        for jp in range(2):
            lo = 2 * g * LANES + jp * LANES
            q128 = q_ref[:, lo:lo + LANES]
            pair = jnp.zeros((TQ, LANES), F32)
            for hh in range(2):
                qm = jnp.where(halves[hh], q128, jnp.zeros_like(q128))
                qms.append(qm)
                s = jnp.where(cmask, _dot_nt(qm, kc), NEG)
                e = jnp.exp(s - jnp.max(s, axis=-1, keepdims=True))
                p = jnp.where(cmask, e / jnp.sum(e, axis=-1, keepdims=True), 0.0)
                psum += p
                pair = jnp.where(halves[hh], _dot(p.astype(BF16), vc), pair)
            ocmp_ref[:, lo:lo + LANES] = pair

        imp = _split_dot(psum, ov_ref[...], passes=3)
        jidx = _lane_iota(imp.shape)
        qblk = (t0 + _row_iota(imp.shape)) // SEL_BLOCK
        score = jnp.where(jidx > qblk, -jnp.inf,
                          jnp.where((jidx == 0) | (jidx > qblk - N_LOCAL_FORCED), jnp.inf, imp))
        sel_t = _topk_mask_t(score.T[0:n_blocks], n_blocks, N_SEL)
        sel_t = jnp.concatenate([sel_t, jnp.zeros((imp.shape[1] - n_blocks, TQ), F32)], axis=0)
        sel = sel_t.T.astype(BF16)
        for c in range(n_keys // KCH):
            selm_ref[c] = _dot(sel, ex_ref[:, c * KCH:(c + 1) * KCH])

        k_rows = slice(g * HEAD_DIM, (g + 1) * HEAD_DIM)
        v_rows = slice(LANES + g * HEAD_DIM, LANES + (g + 1) * HEAD_DIM)

        def body(c, carry):
            kt = _twice(jnp.concatenate([slc_ref[0, c * KCH_CHUNKS + u, k_rows, :]
                                         for u in range(KCH_CHUNKS)], axis=1))
            vt = _twice(jnp.concatenate([slc_ref[0, c * KCH_CHUNKS + u, v_rows, :]
                                         for u in range(KCH_CHUNKS)], axis=1))
            valid = (selm_ref[c] > 0.5) & ((c * KCH + _lane_iota((TQ, KCH))) <= t_k)
            out = []
            for h in range(NSA_GROUP):
                m, l, acc = carry[h]
                s = jnp.where(valid, _dot(qms[h], kt), NEG)
                m_new = jnp.maximum(m, jnp.max(s, axis=-1, keepdims=True))
                alpha = jnp.exp(m - m_new)
                e = jnp.exp(s - m_new)
                l = alpha * l + jnp.sum(e, axis=-1, keepdims=True)
                acc = alpha * acc + _dot_nt(e.astype(BF16), vt)
                out.append((m_new, l, acc))
            return tuple(out)

        init = tuple((jnp.full((TQ, 1), NEG, F32), jnp.zeros((TQ, 1), F32), jnp.zeros((TQ, LANES), F32))
                     for _ in range(NSA_GROUP))
        res = lax.fori_loop(0, n_active, body, init)
        for jp in range(2):
            lo = 2 * g * LANES + jp * LANES
            o0 = res[2 * jp][2] / res[2 * jp][1]
            o1 = res[2 * jp + 1][2] / res[2 * jp + 1][1]
            oslc_ref[:, lo:lo + LANES] = jnp.where(halves[0], o0, o1)


def _nsa_prompt(q, kc, vc, slcb, batch, seq):
    n_cmp = seq // CHUNK
    n_valid = (seq - CMP_BLOCK) // CMP_STRIDE + 1
    n_blocks = seq // SEL_BLOCK
    ov = _overlap_matrix(n_valid, n_cmp, n_blocks, LANES)
    ex = _expand_matrix(LANES, seq)
    nq = seq // TQ
    per_b = lambda rows, w: pl.BlockSpec((rows, w), lambda b, i: (b, 0))
    qspec = pl.BlockSpec((TQ, Q_NSA_W), lambda b, i: (b * nq + i, 0))
    const = lambda a: pl.BlockSpec(a.shape, lambda b, i: (0, 0))
    return pl.pallas_call(
        _nsa_prompt_kernel,
        grid=(batch, nq),
        in_specs=[qspec, per_b(n_cmp, 2 * LANES), per_b(n_cmp, 2 * LANES),
                  pl.BlockSpec((1,) + slcb.shape[1:], lambda b, i: (b, 0, 0, 0)), const(ov), const(ex)],
        out_specs=[qspec, qspec],
        out_shape=[jax.ShapeDtypeStruct((batch * seq, Q_NSA_W), F32)] * 2,
        scratch_shapes=[pltpu.VMEM((seq // KCH, TQ, KCH), F32)],
        compiler_params=pltpu.CompilerParams(dimension_semantics=("parallel", "arbitrary"),
                                             vmem_limit_bytes=VMEM_LIMIT),
        name="nsa_prompt",
    )(q, kc, vc, slcb, ov, ex)


WSPAN = WINDOW + TQ
WSPAN_CHUNKS = WSPAN // KT


def _win_prompt_kernel(q_ref, kv_ref, o_ref):
    t0 = pl.program_id(1) * TQ
    n_chunks = kv_ref.shape[1]
    c0 = jnp.clip(t0 // KT - WINDOW // KT, 0, n_chunks - WSPAN_CHUNKS)
    t = t0 + _row_iota((TQ, WSPAN))
    kpos = c0 * KT + _lane_iota((TQ, WSPAN))
    valid = (kpos <= t) & (kpos >= t - WINDOW)
    halves = _half_masks((TQ, LANES))
    for g in range(NSA_KV_HEADS):
        kt = _twice(jnp.concatenate([kv_ref[0, c0 + u, g * HEAD_DIM:(g + 1) * HEAD_DIM, :]
                                     for u in range(WSPAN_CHUNKS)], axis=1))
        vt = _twice(jnp.concatenate([kv_ref[0, c0 + u, LANES + g * HEAD_DIM:LANES + (g + 1) * HEAD_DIM, :]
                                     for u in range(WSPAN_CHUNKS)], axis=1))
        for jp in range(2):
            lo = 2 * g * LANES + jp * LANES
            q128 = q_ref[:, lo:lo + LANES]
            pair = jnp.zeros((TQ, LANES), F32)
            for hh in range(2):
                qm = jnp.where(halves[hh], q128, jnp.zeros_like(q128))
                s = jnp.where(valid, _dot(qm, kt), NEG)
                e = jnp.exp(s - jnp.max(s, axis=-1, keepdims=True))
                p = e / jnp.sum(e, axis=-1, keepdims=True)
                pair = jnp.where(halves[hh], _dot_nt(p.astype(BF16), vt), pair)
            o_ref[:, lo:lo + LANES] = pair


def _win_prompt(q, winb, batch, seq):
    nq = seq // TQ
    qspec = pl.BlockSpec((TQ, Q_NSA_W), lambda b, i: (b * nq + i, 0))
    return pl.pallas_call(
        _win_prompt_kernel,
        grid=(batch, nq),
        in_specs=[qspec, pl.BlockSpec((1,) + winb.shape[1:], lambda b, i: (b, 0, 0, 0))],
        out_specs=qspec,
        out_shape=jax.ShapeDtypeStruct((batch * seq, Q_NSA_W), F32),
        compiler_params=pltpu.CompilerParams(dimension_semantics=("parallel", "arbitrary"),
                                             vmem_limit_bytes=VMEM_LIMIT),
        name="win_prompt",
    )(q, winb)


SB_TQ = 256
SB_KB = 256
SB_KB_CHUNKS = SB_KB // KT
F32_EXP_ZERO_BELOW = -104.0


def _suffix_matrix(n, n_total_cols):
    j = np.arange(n)[:, None]
    s = np.arange(n)[None, :]
    return jnp.asarray(np.concatenate([j >= s, np.ones((n, n_total_cols), bool)], axis=1), BF16)


def _log1m_sigmoid(z):
    return -jnp.maximum(z, 0.0) - jnp.log1p(jnp.exp(-jnp.abs(z)))


def _sb_prompt_kernel(q_ref, k_ref, v_ref, u_ref, o_ref):
    i = pl.program_id(2)
    halves = _half_masks((SB_TQ, LANES))
    q128 = q_ref[...]
    qms = [jnp.where(h, q128, jnp.zeros_like(q128)) for h in halves]
    u = u_ref[...]
    zero = jnp.zeros((SB_TQ, LANES), F32)

    def block(kb, carry, diagonal):
        kt = jnp.concatenate([k_ref[0, kb * SB_KB_CHUNKS + u_, :, :] for u_ in range(SB_KB_CHUNKS)], axis=1)
        vt = jnp.concatenate([v_ref[0, kb * SB_KB_CHUNKS + u_, :, :] for u_ in range(SB_KB_CHUNKS)], axis=1)
        if diagonal:
            mask = _lane_iota((SB_TQ, SB_KB)) < _row_iota((SB_TQ, SB_KB))
        out = []
        for hh in range(2):
            c, acc = carry[hh]
            z = _dot(qms[hh], kt)
            l1m = _log1m_sigmoid(z)
            if diagonal:
                l1m = jnp.where(mask, l1m, 0.0)
            res = _split_dot(l1m, u, passes=2)
            c_full = jnp.concatenate([c] * (SB_KB // LANES), axis=1)
            a = jnp.exp(z + res[:, :SB_KB] + c_full)
            if diagonal:
                a = jnp.where(mask, a, 0.0)
            out.append((c + res[:, SB_KB:], acc + _dot_nt(a.astype(BF16), vt)))
        return tuple(out)

    carry = block(i, ((zero, zero), (zero, zero)), True)

    def cond(state):
        n, go = state[0], state[1]
        return (n <= i) & go

    def body(state):
        n = state[0]
        carry = block(i - n, state[2], False)
        cmax = jnp.max(jnp.maximum(carry[0][0], carry[1][0]))
        return n + 1, cmax >= F32_EXP_ZERO_BELOW, carry

    _, _, carry = lax.while_loop(cond, body, (jnp.int32(1), jnp.bool_(True), carry))
    o_ref[...] = jnp.where(halves[0], carry[0][1], carry[1][1])


def _sb_prompt(qsb, sbb, batch, seq):
    nq = seq // SB_TQ
    npair = SB_HEADS // 2
    u = _suffix_matrix(SB_KB, LANES)
    n_chunks = sbb.shape[1]
    kspec = pl.BlockSpec((1, n_chunks, LANES, KT), lambda b, j, i: (b, 0, j, 0))
    vspec = pl.BlockSpec((1, n_chunks, LANES, KT), lambda b, j, i: (b, 0, npair + j, 0))
    qspec = pl.BlockSpec((SB_TQ, LANES), lambda b, j, i: (b * nq + i, j))
    return pl.pallas_call(
        _sb_prompt_kernel,
        grid=(batch, npair, nq),
        in_specs=[qspec, kspec, vspec, pl.BlockSpec(u.shape, lambda b, j, i: (0, 0))],
        out_specs=qspec,
        out_shape=jax.ShapeDtypeStruct((batch * seq, SB_QW), F32),
        compiler_params=pltpu.CompilerParams(dimension_semantics=("parallel", "parallel", "arbitrary"),
                                             vmem_limit_bytes=VMEM_LIMIT),
        name="sb_prompt",
    )(qsb, sbb, sbb, u)


N_FF_CHUNKS = 2
FF_CHUNK = D_FF // N_FF_CHUNKS


def _gate_expand_matrices():
    m = np.zeros((3, LANES, Q_NSA_W), np.float32)
    for b in range(3):
        for h in range(NSA_HEADS):
            m[b, b * NSA_HEADS + h, h * HEAD_DIM:(h + 1) * HEAD_DIM] = 1.0
    return jnp.asarray(m, BF16)


def _head_rms(o, gain, bd):
    cols = []
    for c in range(o.shape[1] // LANES):
        oc = o[:, c * LANES:(c + 1) * LANES]
        ms = _split_dot(oc * oc, bd, passes=2)
        cols.append(oc * lax.rsqrt(ms + EPS))
    return jnp.concatenate(cols, axis=1) * gain


def _finish_kernel(x_ref, gates_ref, ocmp_ref, oslc_ref, owin_ref, osb_ref, gexp_ref, gon_ref, gos_ref,
                   bd_ref, wout_ref, gffn_ref, wg_ref, wu_ref, wd_ref, y_ref, hn_ref):
    @pl.when(pl.program_id(1) == 0)
    def _():
        gates = gates_ref[...]
        bd = bd_ref[...]
        o_nsa = (_split_dot(gates, gexp_ref[0]) * ocmp_ref[...]
                 + _split_dot(gates, gexp_ref[1]) * oslc_ref[...]
                 + _split_dot(gates, gexp_ref[2]) * owin_ref[...])
        mix = jnp.concatenate([_head_rms(o_nsa, gon_ref[...], bd), _head_rms(osb_ref[...], gos_ref[...], bd)],
                              axis=1)
        h = x_ref[...] + _dot(mix.astype(BF16), wout_ref[...])
        hn = h * lax.rsqrt(jnp.mean(h * h, axis=-1, keepdims=True) + EPS) * gffn_ref[...]
        hn_ref[...] = hn.astype(BF16)
        y_ref[...] = h

    hn = hn_ref[...]
    gate = _dot(hn, wg_ref[...])
    up = _dot(hn, wu_ref[...])
    act = gate * jax.nn.sigmoid(gate) * up
    y_ref[...] += _dot(act.astype(BF16), wd_ref[...])


def _finish(x2d, gates, o_cmp, o_slc, o_win, o_sb, tm, g_on, g_os, w_out, g_ffn, w_gate, w_up, w_down):
    n = x2d.shape[0]
    gexp = _gate_expand_matrices()
    bd = _head_mean_matrix()
    gon = g_on.reshape(1, Q_NSA_W)
    gos = g_os.reshape(1, SB_QW)
    gffn = g_ffn.reshape(1, D_MODEL)
    row = lambda w: pl.BlockSpec((tm, w), lambda i, c: (i, 0))
    const = lambda a: pl.BlockSpec(a.shape, lambda i, c: (0,) * a.ndim)
    return pl.pallas_call(
        _finish_kernel,
        grid=(n // tm, N_FF_CHUNKS),
        in_specs=[row(D_MODEL), row(LANES), row(Q_NSA_W), row(Q_NSA_W), row(Q_NSA_W), row(SB_QW),
                  const(gexp), const(gon), const(gos), const(bd), const(w_out), const(gffn),
                  pl.BlockSpec((D_MODEL, FF_CHUNK), lambda i, c: (0, c)),
                  pl.BlockSpec((D_MODEL, FF_CHUNK), lambda i, c: (0, c)),
                  pl.BlockSpec((FF_CHUNK, D_MODEL), lambda i, c: (c, 0))],
        out_specs=row(D_MODEL),
        out_shape=jax.ShapeDtypeStruct((n, D_MODEL), F32),
        scratch_shapes=[pltpu.VMEM((tm, D_MODEL), BF16)],
        compiler_params=pltpu.CompilerParams(dimension_semantics=("parallel", "arbitrary"),
                                             vmem_limit_bytes=VMEM_LIMIT),
        name="finish",
    )(x2d, gates, o_cmp, o_slc, o_win, o_sb, gexp, gon, gos, bd, w_out, gffn, w_gate, w_up, w_down)


ROWS8 = SUBLANES
PAGE = 128


def _decode_q(q_row, g):
    shape = (ROWS8, LANES)
    row = _row_iota(shape)
    lane = _lane_iota(shape)
    lo = 2 * g * LANES
    base = jnp.where(row < 2, jnp.broadcast_to(q_row[:, lo:lo + LANES], shape),
                     jnp.broadcast_to(q_row[:, lo + LANES:lo + 2 * LANES], shape))
    val = jnp.where((row % 2) == g, base, pltpu.roll(base, HEAD_DIM, 1))
    keep = (row < NSA_GROUP) & ((lane >= HEAD_DIM) == (g == 1))
    return jnp.where(keep, val, 0.0).astype(BF16)


def _decode_place(o, g):
    return jnp.where((_row_iota(o.shape) % 2) == g, o, pltpu.roll(o, HEAD_DIM, 1))


def _decode_store(o_ref, val, g):
    low = _lane_iota((1, LANES)) < HEAD_DIM
    for jp in range(2):
        lo = 2 * g * LANES + jp * LANES
        o_ref[0, 0:1, lo:lo + LANES] = jnp.where(low, val[2 * jp:2 * jp + 1], val[2 * jp + 1:2 * jp + 2])


def _decode_attend(qg, kt, vt, valid, knew, vnew):
    s = _dot(qg, kt)
    if valid is not None:
        s = jnp.where(valid, s, NEG)
    kn = jnp.broadcast_to(knew, (LANES, LANES)).astype(BF16)
    vn = jnp.broadcast_to(vnew, (LANES, LANES)).astype(BF16)
    s2 = jnp.where(_lane_iota((ROWS8, LANES)) == 0, _dot_nt(qg, kn), NEG)
    m = jnp.maximum(jnp.max(s, axis=-1, keepdims=True), jnp.max(s2, axis=-1, keepdims=True))
    e = jnp.exp(s - m)
    e2 = jnp.exp(s2 - m)
    l = jnp.sum(e, axis=-1, keepdims=True) + jnp.sum(e2, axis=-1, keepdims=True)
    return (_dot_nt(e.astype(BF16), vt) + _dot(e2.astype(BF16), vn)) / l


def _cmp_sample_kernel(pt_ref, q_ref, cache_ref, pe_ref, w1_ref, w2_ref, ov_ref,
                       ocmp_ref, imp_ref, stage_ref, kbuf_ref, vbuf_ref, sem):
    b = pl.program_id(0)
    n_pages = stage_ref.shape[0]

    def page_copy(p):
        return pltpu.make_async_copy(cache_ref.at[pt_ref[b, p], pl.ds(0, 2 * LANES), :], stage_ref.at[p], sem)

    for p in range(n_pages):
        page_copy(p).start()
    for p in range(n_pages):
        page_copy(p).wait()
    for p in range(n_pages):
        kbuf_ref[p * PAGE:(p + 1) * PAGE, :] = stage_ref[p, 0:LANES, :].T
        vbuf_ref[p * PAGE:(p + 1) * PAGE, :] = stage_ref[p, LANES:2 * LANES, :].T

    n_chunks = kbuf_ref.shape[0] // CHUNK
    kc = _compress_rows(kbuf_ref, n_chunks, 0, pe_ref, w1_ref, w2_ref).astype(BF16)
    vc = _compress_rows(vbuf_ref, n_chunks, 1, pe_ref, w1_ref, w2_ref).astype(BF16)
    q_row = q_ref[0]
    cmask = _lane_iota((ROWS8, n_chunks)) < (n_chunks - 1)
    ocmp_ref[...] = jnp.zeros_like(ocmp_ref)
    imp_ref[...] = jnp.zeros_like(imp_ref)
    for g in range(NSA_KV_HEADS):
        qg = _decode_q(q_row, g)
        s = jnp.where(cmask, _dot_nt(qg, kc[:, g * LANES:(g + 1) * LANES]), NEG)
        e = jnp.exp(s - jnp.max(s, axis=-1, keepdims=True))
        p = jnp.where(cmask, e / jnp.sum(e, axis=-1, keepdims=True), 0.0)
        o = _dot(p.astype(BF16), vc[:, g * LANES:(g + 1) * LANES])
        _decode_store(ocmp_ref, _decode_place(o, g), g)
        psum = jnp.sum(jnp.where(_row_iota(p.shape) < NSA_GROUP, p, 0.0), axis=0, keepdims=True)
        imp = _split_dot(jnp.broadcast_to(psum, p.shape), ov_ref[...], passes=3)
        imp_ref[0, g:g + 1, :] = imp[0:1]


def _cmp_sample(page_table, q3, cache_nsa_t, pe, w1, w2, past_len):
    db, n_pages = page_table.shape
    n_chunks = past_len // CHUNK
    n_blocks = past_len // SEL_BLOCK + 1
    ncol = 2 * LANES
    assert n_blocks <= ncol
    ov = _overlap_matrix(n_chunks - 1, n_chunks, n_blocks, ncol)
    const = lambda a: pl.BlockSpec(a.shape, lambda b, pt: (0,) * a.ndim)
    gs = pltpu.PrefetchScalarGridSpec(
        num_scalar_prefetch=1,
        grid=(db,),
        in_specs=[pl.BlockSpec((1, 1, Q_NSA_W), lambda b, pt: (b, 0, 0)),
                  pl.BlockSpec(memory_space=pl.ANY), const(pe), const(w1), const(w2), const(ov)],
        out_specs=[pl.BlockSpec((1, ROWS8, Q_NSA_W), lambda b, pt: (b, 0, 0)),
                   pl.BlockSpec((1, ROWS8, ncol), lambda b, pt: (b, 0, 0))],
        scratch_shapes=[pltpu.VMEM((n_pages, 2 * LANES, PAGE), F32),
                        pltpu.VMEM((past_len, LANES), F32), pltpu.VMEM((past_len, LANES), F32),
                        pltpu.SemaphoreType.DMA(())],
    )
    return pl.pallas_call(
        _cmp_sample_kernel,
        grid_spec=gs,
        out_shape=[jax.ShapeDtypeStruct((db, ROWS8, Q_NSA_W), F32), jax.ShapeDtypeStruct((db, ROWS8, ncol), F32)],
        compiler_params=pltpu.CompilerParams(dimension_semantics=("arbitrary",), vmem_limit_bytes=VMEM_LIMIT),
        name="cmp_sample",
    )(page_table, q3, cache_nsa_t, pe, w1, w2, ov)


def _topk_sample_kernel(imp_ref, idx_ref, *, n_blocks, q_block):
    imp = imp_ref[...]
    jidx = _lane_iota(imp.shape)
    score = jnp.where(jidx > q_block, -jnp.inf,
                      jnp.where((jidx == 0) | (jidx > q_block - N_LOCAL_FORCED), jnp.inf, imp))
    nb_pad = -(-n_blocks // SUBLANES) * SUBLANES
    st = score.T[0:nb_pad]
    jrow = _row_iota(st.shape)
    st = jnp.where(jrow < n_blocks, st, -jnp.inf)
    cnt = jnp.zeros(st.shape, F32)
    for i in range(n_blocks):
        ri = st[i:i + 1, :]
        cnt += jnp.where((ri > st) | ((ri == st) & (i < jrow)), 1.0, 0.0)
    jf = jrow.astype(F32)
    rows = []
    for k in range(N_SEL):
        hit = (cnt == k) & (st > -jnp.inf)
        rows.append(jnp.sum(jnp.where(hit, jf, 0.0), axis=0, keepdims=True)
                    + jnp.where(jnp.sum(jnp.where(hit, 1.0, 0.0), axis=0, keepdims=True) > 0.5, 0.0, -1.0))
    idx_ref[...] = jnp.concatenate(rows, axis=0).astype(jnp.int32)


def _topk_sample(imp2d, n_blocks, q_block):
    rows = imp2d.shape[0]
    return pl.pallas_call(
        functools.partial(_topk_sample_kernel, n_blocks=n_blocks, q_block=q_block),
        out_shape=jax.ShapeDtypeStruct((N_SEL, rows), jnp.int32),
        compiler_params=pltpu.CompilerParams(vmem_limit_bytes=VMEM_LIMIT),
        name="topk_sample",
    )(imp2d)


def _slc_sample_kernel(pt_ref, idx_ref, q_ref, new_ref, cache_ref, o_ref, buf_ref, sem, *, n_past_blocks):
    b = pl.program_id(0)
    per_page = PAGE // SEL_BLOCK

    def block_index(g, k):
        return idx_ref[(b * NSA_KV_HEADS + g) * N_SEL + k]

    def page_copy(g, k):
        j = jnp.clip(block_index(g, k), 0, n_past_blocks - 1)
        return pltpu.make_async_copy(cache_ref.at[pt_ref[b, j // per_page], pl.ds(2 * LANES, 2 * LANES), :],
                                     buf_ref.at[g, k], sem)

    for g in range(NSA_KV_HEADS):
        for k in range(N_SEL):
            page_copy(g, k).start()
    for g in range(NSA_KV_HEADS):
        for k in range(N_SEL):
            page_copy(g, k).wait()

    q_row = q_ref[0]
    new = new_ref[0]
    o_ref[...] = jnp.zeros_like(o_ref)
    n_keys = N_SEL * PAGE
    lane = _lane_iota((ROWS8, n_keys))
    slot = lane // PAGE
    half = (lane % PAGE) // SEL_BLOCK
    for g in range(NSA_KV_HEADS):
        valid = jnp.zeros((ROWS8, n_keys), jnp.bool_)
        for k in range(N_SEL):
            j = block_index(g, k)
            valid = valid | ((slot == k) & (half == j % per_page) & ((j >= 0) & (j < n_past_blocks)))
        kt = jnp.concatenate([buf_ref[g, k, 0:LANES, :] for k in range(N_SEL)], axis=1).astype(BF16)
        vt = jnp.concatenate([buf_ref[g, k, LANES:2 * LANES, :] for k in range(N_SEL)], axis=1).astype(BF16)
        o = _decode_attend(_decode_q(q_row, g), kt, vt, valid,
                           new[:, 2 * LANES:3 * LANES], new[:, 3 * LANES:4 * LANES])
        _decode_store(o_ref, _decode_place(o, g), g)


def _slc_sample(page_table, idx_flat, q3, nsa_new3, cache_nsa_t, past_len):
    db = page_table.shape[0]
    gs = pltpu.PrefetchScalarGridSpec(
        num_scalar_prefetch=2,
        grid=(db,),
        in_specs=[pl.BlockSpec((1, 1, Q_NSA_W), lambda b, pt, ix: (b, 0, 0)),
                  pl.BlockSpec((1, 1, 4 * LANES), lambda b, pt, ix: (b, 0, 0)),
                  pl.BlockSpec(memory_space=pl.ANY)],
        out_specs=pl.BlockSpec((1, ROWS8, Q_NSA_W), lambda b, pt, ix: (b, 0, 0)),
        scratch_shapes=[pltpu.VMEM((NSA_KV_HEADS, N_SEL, 2 * LANES, PAGE), F32),
                        pltpu.SemaphoreType.DMA(())],
    )
    return pl.pallas_call(
        functools.partial(_slc_sample_kernel, n_past_blocks=past_len // SEL_BLOCK),
        grid_spec=gs,
        out_shape=jax.ShapeDtypeStruct((db, ROWS8, Q_NSA_W), F32),
        compiler_params=pltpu.CompilerParams(dimension_semantics=("arbitrary",), vmem_limit_bytes=VMEM_LIMIT),
        name="slc_sample",
    )(page_table, idx_flat, q3, nsa_new3, cache_nsa_t)


def _win_sample_kernel(q_ref, new_ref, win_ref, o_ref):
    q_row = q_ref[0]
    new = new_ref[0]
    o_ref[...] = jnp.zeros_like(o_ref)
    for g in range(NSA_KV_HEADS):
        o = _decode_attend(_decode_q(q_row, g), win_ref[0, 0:LANES, :].astype(BF16),
                           win_ref[0, LANES:2 * LANES, :].astype(BF16), None,
                           new[:, 0:LANES], new[:, LANES:2 * LANES])
        _decode_store(o_ref, _decode_place(o, g), g)


def _win_sample(q3, win_new3, state_win_t):
    db, w, wb = state_win_t.shape
    return pl.pallas_call(
        _win_sample_kernel,
        grid=(db,),
        in_specs=[pl.BlockSpec((1, 1, Q_NSA_W), lambda b: (b, 0, 0)),
                  pl.BlockSpec((1, 1, w), lambda b: (b, 0, 0)),
                  pl.BlockSpec((1, w, wb), lambda b: (b, 0, 0))],
        out_specs=pl.BlockSpec((1, ROWS8, Q_NSA_W), lambda b: (b, 0, 0)),
        out_shape=jax.ShapeDtypeStruct((db, ROWS8, Q_NSA_W), F32),
        compiler_params=pltpu.CompilerParams(dimension_semantics=("parallel",), vmem_limit_bytes=VMEM_LIMIT),
        name="win_sample",
    )(q3, win_new3, state_win_t)


SB_GROUP_PAGES = 8
SB_GROUP_ROWS = SB_GROUP_PAGES * PAGE


def _sb_sample_kernel(pt_ref, q_ref, cache_ref, u_ref, o_ref, buf_ref, sem):
    b = pl.program_id(0)
    n_groups = pt_ref.shape[1] // SB_GROUP_PAGES

    def page_copy(grp, slot, p):
        return pltpu.make_async_copy(cache_ref.at[pt_ref[b, grp * SB_GROUP_PAGES + p]],
                                     buf_ref.at[slot, p], sem.at[slot])

    def start_group(grp, slot):
        for p in range(SB_GROUP_PAGES):
            page_copy(grp, slot, p).start()

    def wait_group(grp, slot):
        for p in range(SB_GROUP_PAGES):
            page_copy(grp, slot, p).wait()

    shape = (ROWS8, SB_QW)
    head_lanes = _row_iota(shape) == (_lane_iota(shape) // HEAD_DIM)
    qbd = jnp.where(head_lanes, jnp.broadcast_to(q_ref[0], shape), 0.0).astype(BF16)
    u = u_ref[...]
    n_ch = SB_GROUP_ROWS // LANES

    start_group(n_groups - 1, (n_groups - 1) % 2)

    def body(n, carry):
        c, acc = carry
        grp = n_groups - 1 - n
        slot = grp % 2

        @pl.when(grp > 0)
        def _():
            start_group(grp - 1, 1 - slot)

        wait_group(grp, slot)
        kt = jnp.concatenate([buf_ref[slot, p, 0:SB_QW, :] for p in range(SB_GROUP_PAGES)], axis=1).astype(BF16)
        vt = jnp.concatenate([buf_ref[slot, p, SB_QW:2 * SB_QW, :] for p in range(SB_GROUP_PAGES)],
                             axis=1).astype(BF16)
        z = _dot(qbd, kt)
        l1m = _log1m_sigmoid(z)
        stacked = jnp.concatenate([l1m[:, i * LANES:(i + 1) * LANES] for i in range(n_ch)], axis=0)
        res = _split_dot(stacked, u, passes=2)
        a_chunks = [None] * n_ch
        for i in reversed(range(n_ch)):
            sl = slice(i * LANES, (i + 1) * LANES)
            a_chunks[i] = jnp.exp(z[:, sl] + res[i * ROWS8:(i + 1) * ROWS8, 0:LANES] + c)
            c = c + res[i * ROWS8:(i + 1) * ROWS8, LANES:2 * LANES]
        a = jnp.concatenate(a_chunks, axis=1).astype(BF16)
        return c, acc + _dot_nt(a, vt)

    c0 = jnp.zeros((ROWS8, LANES), F32)
    _, acc = lax.fori_loop(0, n_groups, body, (c0, jnp.zeros(shape, F32)))
    o_ref[...] = jnp.zeros_like(o_ref)
    o_ref[0, 0:1, :] = jnp.sum(jnp.where(head_lanes, acc, 0.0), axis=0, keepdims=True)


def _sb_sample(page_table, qsb3, cache_sb_t):
    db = page_table.shape[0]
    w = cache_sb_t.shape[1]
    u = _suffix_matrix(LANES, LANES)
    gs = pltpu.PrefetchScalarGridSpec(
        num_scalar_prefetch=1,
        grid=(db,),
        in_specs=[pl.BlockSpec((1, 1, SB_QW), lambda b, pt: (b, 0, 0)),
                  pl.BlockSpec(memory_space=pl.ANY),
                  pl.BlockSpec(u.shape, lambda b, pt: (0, 0))],
        out_specs=pl.BlockSpec((1, ROWS8, SB_QW), lambda b, pt: (b, 0, 0)),
        scratch_shapes=[pltpu.VMEM((2, SB_GROUP_PAGES, w, PAGE), F32), pltpu.SemaphoreType.DMA((2,))],
    )
    return pl.pallas_call(
        _sb_sample_kernel,
        grid_spec=gs,
        out_shape=jax.ShapeDtypeStruct((db, ROWS8, SB_QW), F32),
        compiler_params=pltpu.CompilerParams(dimension_semantics=("arbitrary",), vmem_limit_bytes=VMEM_LIMIT),
        name="sb_sample",
    )(page_table, qsb3, cache_sb_t, u)


PROMPT_TM = 256


def _feature_major(x, n_lead):
    perm = tuple(range(n_lead)) + tuple(range(n_lead + 1, x.ndim)) + (n_lead,)
    xt = x.transpose(perm)
    return xt.reshape(xt.shape[:n_lead] + (-1, xt.shape[-1]))


def _token_major(xt, feat_shape):
    b, _, t = xt.shape
    nd = len(feat_shape)
    return xt.reshape((b,) + feat_shape + (t,)).transpose((0, nd + 1) + tuple(range(1, nd + 1)))


def kernel(x_prompt, x_sample, cache_nsa_kv, cache_sb_kv, state_nsa_window, page_table, g_attn, w_in, g_q, g_k,
           w_cmp1_k, w_cmp2_k, pe_cmp_k, w_cmp1_v, w_cmp2_v, pe_cmp_v, g_out_nsa, g_out_sb, w_out, g_ffn,
           w_gate_up, w_down):
    depth = w_in.shape[0]
    assert depth == 1
    batch, seq, _ = x_prompt.shape
    db, dseq, _ = x_sample.shape
    assert dseq == 1
    page = cache_nsa_kv.shape[2]
    past_len = page_table.shape[1] * page
    wb = state_nsa_window.shape[2]
    assert page == PAGE and wb == WINDOW

    l = 0
    w_packed = _pack_w_in(w_in[l])
    pe, w1, w2 = _pack_cmp_weights(w_cmp1_k[l], w_cmp2_k[l], pe_cmp_k[l], w_cmp1_v[l], w_cmp2_v[l], pe_cmp_v[l])
    fin_w = (g_out_nsa[l], g_out_sb[l], w_out[l].astype(BF16), g_ffn[l],
             w_gate_up[l][:, :D_FF].astype(BF16), w_gate_up[l][:, D_FF:].astype(BF16), w_down[l].astype(BF16))

    xp = x_prompt.reshape(batch * seq, D_MODEL)
    (q, gates, cmp_rows, qsb, nsa_t, win_t, sbkv_t, slcb, winb, sbb) = _project_prompt(
        xp, batch, seq, PROMPT_TM, g_attn[l], w_packed, g_q[l], g_k[l])
    kc, vc = _compress_prompt(cmp_rows, seq, pe, w1, w2)
    o_cmp, o_slc = _nsa_prompt(q, kc, vc, slcb, batch, seq)
    o_win = _win_prompt(q, winb, batch, seq)
    o_sb = _sb_prompt(qsb, sbb, batch, seq)
    yp = _finish(xp, gates, o_cmp, o_slc, o_win, o_sb, PROMPT_TM, *fin_w)

    xs = x_sample.reshape(db, D_MODEL)
    q_s, gates_s, nsa_s, win_s, qsb_s, sbkv_s = _project_sample(xs, past_len, g_attn[l], w_packed, g_q[l], g_k[l])
    q3 = q_s.reshape(db, 1, Q_NSA_W)
    cache_nsa_t = _feature_major(cache_nsa_kv[l], 1)
    cache_sb_t = _feature_major(cache_sb_kv[l], 1)
    s_win_t = _feature_major(state_nsa_window[l], 1)
    o_cmp_s, imp = _cmp_sample(page_table, q3, cache_nsa_t, pe, w1, w2, past_len)
    n_blocks = past_len // SEL_BLOCK + 1
    idx = _topk_sample(imp.reshape(db * ROWS8, imp.shape[-1]), n_blocks, past_len // SEL_BLOCK)
    idx = idx.reshape(N_SEL, db, ROWS8)[:, :, :NSA_KV_HEADS].transpose(1, 2, 0).reshape(-1)
    o_slc_s = _slc_sample(page_table, idx, q3, nsa_s.reshape(db, 1, 4 * LANES), cache_nsa_t, past_len)
    o_win_s = _win_sample(q3, win_s.reshape(db, 1, 2 * LANES), s_win_t)
    o_sb_s = _sb_sample(page_table, qsb_s.reshape(db, 1, SB_QW), cache_sb_t)
    ys = _finish(xs, gates_s, o_cmp_s[:, 0], o_slc_s[:, 0], o_win_s[:, 0], o_sb_s[:, 0], db, *fin_w)

    kvh = (NSA_KV_HEADS, HEAD_DIM)
    keep = min(WINDOW, seq)
    win_all_t = jnp.concatenate([s_win_t[:, :, dseq:], win_s.reshape(db, dseq, 2 * LANES).transpose(0, 2, 1)], axis=2)
    return (yp.reshape(batch, seq, D_MODEL), ys.reshape(db, dseq, D_MODEL),
            _token_major(nsa_t, (4,) + kvh)[None], nsa_s.reshape((depth, db, dseq, 4) + kvh),
            _token_major(sbkv_t, (2, SB_HEADS, HEAD_DIM))[None],
            sbkv_s.reshape(depth, db, dseq, 2, SB_HEADS, HEAD_DIM),
            _token_major(win_t[:, :, seq - keep:], (2,) + kvh)[None],
            _token_major(win_all_t, (2,) + kvh)[None])
```

```python
import functools
import math

import numpy as np
import jax
import jax.numpy as jnp
from jax import lax
from jax.experimental import pallas as pl
from jax.experimental.pallas import tpu as pltpu

F32 = jnp.float32
BF16 = jnp.bfloat16

D_MODEL = 1024
HEAD_DIM = 64
NSA_HEADS = 8
SB_HEADS = 8
NSA_KV_HEADS = 2
NSA_GROUP = NSA_HEADS // NSA_KV_HEADS
CMP_BLOCK = 32
CMP_STRIDE = 16
CMP_HIDDEN = 128
SEL_BLOCK = 64
N_SEL = 16
N_LOCAL_FORCED = 2
WINDOW = 512
D_FF = 2816
ROPE_THETA = 10000.0
EPS = 1e-6
NEG = -1e30
SCALE = HEAD_DIM ** -0.5

Q_NSA_W = NSA_HEADS * HEAD_DIM
KV_NSA_W = 3 * 2 * NSA_KV_HEADS * HEAD_DIM
GATE_W = 3 * NSA_HEADS
SB_QW = SB_HEADS * HEAD_DIM
SB_W = 3 * SB_QW
OFF_KV = Q_NSA_W
OFF_GATE = OFF_KV + KV_NSA_W
OFF_SB = OFF_GATE + GATE_W

LANES = 128
SUBLANES = 8
VMEM_LIMIT = 56 * 1024 * 1024

P_Q = 0
P_KV = P_Q + Q_NSA_W
P_SB = P_KV + KV_NSA_W
P_GATE = P_SB + SB_W
P_W = P_GATE + LANES

KT = LANES


def _dot(a, b):
    return jnp.dot(a, b, preferred_element_type=F32)


def _dot_nt(a, b):
    return lax.dot_general(a, b, (((1,), (1,)), ((), ())), preferred_element_type=F32)


def _split_dot(x, w, passes=2):
    acc = None
    rem = x
    for _ in range(passes):
        piece = rem.astype(BF16)
        term = _dot(piece, w)
        acc = term if acc is None else acc + term
        rem = rem - piece.astype(F32)
    return acc


def _lane_iota(shape):
    return lax.broadcasted_iota(jnp.int32, shape, len(shape) - 1)


def _row_iota(shape):
    return lax.broadcasted_iota(jnp.int32, shape, 0)


def _half_masks(shape):
    lane = _lane_iota(shape)
    return [lane < HEAD_DIM, lane >= HEAD_DIM]


def _twice(x):
    return jnp.concatenate([x, x], axis=0)


def _head_norm_rope(s, gain, cos, sin, bd):
    ms = _dot((s * s).astype(BF16), bd)
    y = s * lax.rsqrt(ms + EPS) * gain
    lane = _lane_iota(y.shape)
    first_half = (lane % HEAD_DIM) < (HEAD_DIM // 2)
    swapped = jnp.where(first_half, pltpu.roll(y, LANES - HEAD_DIM // 2, 1),
                        pltpu.roll(y, HEAD_DIM // 2, 1))
    return y * cos + swapped * sin


def _proj_sections(x_ref, gattn_ref, w_ref, cos_ref, sin_ref, gains_ref, bd_ref):
    x = x_ref[...]
    xn = x * lax.rsqrt(jnp.mean(x * x, axis=-1, keepdims=True) + EPS) * gattn_ref[...]
    xb = xn.astype(BF16)
    cos = cos_ref[...]
    sin = sin_ref[...]
    bd = bd_ref[...]
    gains = gains_ref[...]
    pq = _dot(xb, w_ref[:, P_Q:P_KV])
    q = [_head_norm_rope(pq[:, c * LANES:(c + 1) * LANES], gains[0:1], cos, sin, bd) * SCALE
         for c in range(Q_NSA_W // LANES)]
    pkv = _dot(xb, w_ref[:, P_KV:P_SB])
    kv = []
    for br in range(3):
        kv.append(_head_norm_rope(pkv[:, 2 * br * LANES:(2 * br + 1) * LANES], gains[1 + br:2 + br], cos, sin, bd))
        kv.append(pkv[:, (2 * br + 1) * LANES:(2 * br + 2) * LANES])
    psb = _dot(xb, w_ref[:, P_SB:P_GATE])
    gates = jax.nn.sigmoid(_dot(xb, w_ref[:, P_GATE:P_W]))
    return q, kv, psb, gates


def _proj_sample_kernel(x_ref, gattn_ref, w_ref, cos_ref, sin_ref, gains_ref, bd_ref,
                        q_ref, gates_ref, nsa_ref, win_ref, qsb_ref, sbkv_ref):
    q, kv, psb, gates = _proj_sections(x_ref, gattn_ref, w_ref, cos_ref, sin_ref, gains_ref, bd_ref)
    for c in range(4):
        q_ref[:, c * LANES:(c + 1) * LANES] = q[c]
        nsa_ref[:, c * LANES:(c + 1) * LANES] = kv[c]
    for c in range(2):
        win_ref[:, c * LANES:(c + 1) * LANES] = kv[4 + c]
    qsb_ref[...] = psb[:, 0:SB_QW] * SCALE
    sbkv_ref[...] = psb[:, SB_QW:SB_W]
    gates_ref[...] = gates


def _proj_prompt_kernel(x_ref, gattn_ref, w_ref, cos_ref, sin_ref, gains_ref, bd_ref,
                        q_ref, gates_ref, cmp_ref, qsb_ref, nsat_ref, wint_ref, sbkvt_ref,
                        slcb_ref, winb_ref, sbb_ref):
    q, kv, psb, gates = _proj_sections(x_ref, gattn_ref, w_ref, cos_ref, sin_ref, gains_ref, bd_ref)
    tm = x_ref.shape[0]
    for c in range(4):
        q_ref[:, c * LANES:(c + 1) * LANES] = q[c].astype(BF16)
    for c in range(2):
        cmp_ref[:, c * LANES:(c + 1) * LANES] = kv[c]
    qsb_ref[...] = (psb[:, 0:SB_QW] * SCALE).astype(BF16)
    gates_ref[...] = gates

    def store_t(f32_ref, bf_ref, row0, bf_row0, xt):
        f32_ref[0, row0:row0 + LANES, :] = xt
        if bf_ref is not None:
            for u in range(tm // KT):
                bf_ref[0, u, bf_row0:bf_row0 + LANES, :] = xt[:, u * KT:(u + 1) * KT].astype(BF16)

    for c in range(4):
        store_t(nsat_ref, slcb_ref if c >= 2 else None, c * LANES, (c - 2) * LANES, kv[c].T)
    for c in range(2):
        store_t(wint_ref, winb_ref, c * LANES, c * LANES, kv[4 + c].T)
    for c in range(2 * SB_QW // LANES):
        store_t(sbkvt_ref, sbb_ref, c * LANES, c * LANES, psb[:, SB_QW + c * LANES:SB_QW + (c + 1) * LANES].T)


def _rope_tables(positions):
    inv = np.exp(-math.log(ROPE_THETA) * np.arange(0, HEAD_DIM, 2, dtype=np.float64) / HEAD_DIM)
    ang = np.asarray(positions, np.float64)[:, None] * inv[None, :]
    cos = np.tile(np.cos(ang), (1, 2 * LANES // HEAD_DIM))
    sin = np.tile(np.concatenate([-np.sin(ang), np.sin(ang)], axis=1), (1, LANES // HEAD_DIM))
    assert cos.shape[1] == LANES and sin.shape[1] == LANES
    return jnp.asarray(cos, F32), jnp.asarray(sin, F32)


def _pack_w_in(w_in):
    gate = w_in[:, OFF_GATE:OFF_SB].reshape(D_MODEL, NSA_HEADS, 3).transpose(0, 2, 1).reshape(D_MODEL, GATE_W)
    gate = jnp.pad(gate, ((0, 0), (0, LANES - GATE_W)))
    return jnp.concatenate([w_in[:, :OFF_GATE], w_in[:, OFF_SB:], gate], axis=1).astype(BF16)


def _head_mean_matrix():
    idx = np.arange(LANES) // HEAD_DIM
    return jnp.asarray((idx[:, None] == idx[None, :]) / HEAD_DIM, BF16)


def _proj_common(positions, tm, tokens_per_seq, g_attn, g_q, g_k):
    cos, sin = _rope_tables(positions)
    if cos.shape[0] == 1:
        cos = jnp.broadcast_to(cos, (tm, LANES))
        sin = jnp.broadcast_to(sin, (tm, LANES))
        tbl_blocks = 1
    else:
        tbl_blocks = tokens_per_seq // tm
    gains = jnp.concatenate([jnp.tile(g_q[None, :], (1, 2)), jnp.tile(g_k, (1, 2)),
                             jnp.zeros((SUBLANES - 4, LANES), F32)], axis=0)
    return cos, sin, tbl_blocks, gains, _head_mean_matrix(), g_attn.reshape(1, D_MODEL)


def _project_sample(x2d, position, g_attn, w_packed, g_q, g_k):
    n = x2d.shape[0]
    cos, sin, _, gains, bd, g2 = _proj_common(np.full((1,), position), n, 1, g_attn, g_q, g_k)
    full = lambda a: pl.BlockSpec(a.shape, lambda i: (0, 0))
    widths = [Q_NSA_W, LANES, 4 * LANES, 2 * LANES, SB_QW, 2 * SB_QW]
    return pl.pallas_call(
        _proj_sample_kernel,
        grid=(1,),
        in_specs=[full(x2d), full(g2), full(w_packed), full(cos), full(sin), full(gains), full(bd)],
        out_specs=[pl.BlockSpec((n, w), lambda i: (0, 0)) for w in widths],
        out_shape=[jax.ShapeDtypeStruct((n, w), F32) for w in widths],
        compiler_params=pltpu.CompilerParams(dimension_semantics=("arbitrary",), vmem_limit_bytes=VMEM_LIMIT),
        name="in_proj_sample",
    )(x2d, g2, w_packed, cos, sin, gains, bd)


def _project_prompt(x2d, batch, seq, tm, g_attn, w_packed, g_q, g_k):
    cos, sin, tbl_blocks, gains, bd, g2 = _proj_common(np.arange(seq), tm, seq, g_attn, g_q, g_k)
    nt = seq // tm
    row = lambda w: pl.BlockSpec((tm, w), lambda i: (i, 0))
    const = lambda a: pl.BlockSpec(a.shape, lambda i: (0, 0))
    feat = lambda f: pl.BlockSpec((1, f, tm), lambda i: (i // nt, 0, i % nt))
    featb = lambda f: pl.BlockSpec((1, tm // KT, f, KT), lambda i: (i // nt, i % nt, 0, 0))
    n = batch * seq
    sds = jax.ShapeDtypeStruct
    out_shape = [sds((n, Q_NSA_W), BF16), sds((n, LANES), F32), sds((n, 2 * LANES), F32), sds((n, SB_QW), BF16),
                 sds((batch, 4 * LANES, seq), F32), sds((batch, 2 * LANES, seq), F32),
                 sds((batch, 2 * SB_QW, seq), F32),
                 sds((batch, seq // KT, 2 * LANES, KT), BF16), sds((batch, seq // KT, 2 * LANES, KT), BF16),
                 sds((batch, seq // KT, 2 * SB_QW, KT), BF16)]
    out_specs = [row(Q_NSA_W), row(LANES), row(2 * LANES), row(SB_QW),
                 feat(4 * LANES), feat(2 * LANES), feat(2 * SB_QW),
                 featb(2 * LANES), featb(2 * LANES), featb(2 * SB_QW)]
    return pl.pallas_call(
        _proj_prompt_kernel,
        grid=(n // tm,),
        in_specs=[row(D_MODEL), const(g2), const(w_packed),
                  pl.BlockSpec((tm, LANES), lambda i: (i % tbl_blocks, 0)),
                  pl.BlockSpec((tm, LANES), lambda i: (i % tbl_blocks, 0)),
                  const(gains), const(bd)],
        out_specs=out_specs,
        out_shape=out_shape,
        compiler_params=pltpu.CompilerParams(dimension_semantics=("parallel",), vmem_limit_bytes=VMEM_LIMIT),
        name="in_proj_prompt",
    )(x2d, g2, w_packed, cos, sin, gains, bd)


CHUNK = CMP_STRIDE


def _compress_rows(rows_ref, n_chunks, kv, pe_ref, w1_ref, w2_ref):
    xs = [rows_ref[pl.ds(r, n_chunks, stride=CHUNK), :] for r in range(CHUNK)]
    first = jnp.concatenate([(x + pe_ref[kv, r:r + 1, :]).astype(BF16) for r, x in enumerate(xs)], axis=1)
    second = jnp.concatenate([(x + pe_ref[kv, CHUNK + r:CHUNK + r + 1, :]).astype(BF16)
                              for r, x in enumerate(xs)], axis=1)
    h = _dot(first, w1_ref[kv, 0]) + pltpu.roll(_dot(second, w1_ref[kv, 1]), n_chunks - 1, 0)
    return _dot(jax.nn.gelu(h, approximate=True).astype(BF16), w2_ref[kv])


def _cmp_prompt_kernel(krows_ref, vrows_ref, pe_ref, w1_ref, w2_ref, kc_ref, vc_ref):
    n_chunks = krows_ref.shape[0] // CHUNK
    kc_ref[...] = _compress_rows(krows_ref, n_chunks, 0, pe_ref, w1_ref, w2_ref).astype(BF16)
    vc_ref[...] = _compress_rows(vrows_ref, n_chunks, 1, pe_ref, w1_ref, w2_ref).astype(BF16)


def _pack_cmp_weights(w1k, w2k, pek, w1v, w2v, pev):
    eye = jnp.eye(NSA_KV_HEADS, dtype=F32)

    def w1_pack(w1):
        w1r = w1.reshape(CMP_BLOCK, HEAD_DIM, CMP_HIDDEN)
        return jnp.einsum('rdj,gh->rgdhj', w1r, eye).reshape(2, CHUNK * LANES, 2 * CMP_HIDDEN)

    def w2_pack(w2):
        return jnp.einsum('jd,gh,u->gjhud', w2, eye, jnp.ones((2,), F32)).reshape(2 * CMP_HIDDEN, 2 * LANES)

    w1 = jnp.stack([w1_pack(w1k), w1_pack(w1v)]).astype(BF16)
    w2 = jnp.stack([w2_pack(w2k), w2_pack(w2v)]).astype(BF16)
    pe = jnp.stack([jnp.tile(pek, (1, 2)), jnp.tile(pev, (1, 2))])
    return pe, w1, w2


def _compress_prompt(cmp_rows, rows_per_step, pe, w1, w2):
    n = cmp_rows.shape[0]
    const = lambda a: pl.BlockSpec(a.shape, lambda i: (0,) * a.ndim)
    cps = rows_per_step // CHUNK
    return pl.pallas_call(
        _cmp_prompt_kernel,
        grid=(n // rows_per_step,),
        in_specs=[pl.BlockSpec((rows_per_step, LANES), lambda i: (i, 0)),
                  pl.BlockSpec((rows_per_step, LANES), lambda i: (i, 1)), const(pe), const(w1), const(w2)],
        out_specs=[pl.BlockSpec((cps, 2 * LANES), lambda i: (i, 0))] * 2,
        out_shape=[jax.ShapeDtypeStruct((n // CHUNK, 2 * LANES), BF16)] * 2,
        compiler_params=pltpu.CompilerParams(dimension_semantics=("parallel",), vmem_limit_bytes=VMEM_LIMIT),
        name="cmp_prompt",
    )(cmp_rows, cmp_rows, pe, w1, w2)


TQ = 128
KCH = 512
KCH_CHUNKS = KCH // KT


def _overlap_matrix(n_valid_cmp, n_rows, n_blocks, n_cols):
    c = np.arange(n_rows)[:, None] * CMP_STRIDE
    j = np.arange(n_cols)[None, :] * SEL_BLOCK
    ov = (c < j + SEL_BLOCK) & (c + CMP_BLOCK > j)
    ov &= (np.arange(n_rows)[:, None] < n_valid_cmp) & (np.arange(n_cols)[None, :] < n_blocks)
    return jnp.asarray(ov, BF16)


def _expand_matrix(n_rows, n_keys):
    j = np.arange(n_rows)[:, None]
    k = np.arange(n_keys)[None, :] // SEL_BLOCK
    return jnp.asarray(j == k, BF16)


def _topk_mask_t(score_t, n_blocks, k):
    jrow = _row_iota(score_t.shape)
    cnt = jnp.zeros(score_t.shape, F32)
    for i in range(n_blocks):
        ri = score_t[i:i + 1, :]
        beats = (ri > score_t) | ((ri == score_t) & (i < jrow))
        cnt += jnp.where(beats, 1.0, 0.0)
    return jnp.where((cnt < k) & (score_t > -jnp.inf), 1.0, 0.0)


def _stack_heads(q_ref, g, halves):
    parts = []
    for jp in range(2):
        lo = 2 * g * LANES + jp * LANES
        q128 = q_ref[:, lo:lo + LANES]
        parts += [jnp.where(h, q128, jnp.zeros_like(q128)) for h in halves]
    return jnp.concatenate(parts, axis=0)


def _unstack_store(o_ref, o, g, halves):
    for jp in range(2):
        lo = 2 * g * LANES + jp * LANES
        o_ref[:, lo:lo + LANES] = jnp.where(halves[0], o[2 * jp * TQ:(2 * jp + 1) * TQ],
                                            o[(2 * jp + 1) * TQ:(2 * jp + 2) * TQ])


def _nsa_prompt_kernel(q_ref, kc_ref, vc_ref, slc_ref, ov_ref, ex_ref, ocmp_ref, oslc_ref, selm_ref, s_ref):
    t0 = pl.program_id(1) * TQ
    n_keys = slc_ref.shape[1] * KT
    n_blocks = n_keys // SEL_BLOCK
    n_cmp = kc_ref.shape[0]
    rows = NSA_GROUP * TQ
    t_c = t0 + _row_iota((rows, n_cmp)) % TQ
    cmask = (_lane_iota((rows, n_cmp)) * CMP_STRIDE + (CMP_BLOCK - 1)) <= t_c
    halves = _half_masks((TQ, LANES))
    t_k = t0 + _row_iota((rows, KCH)) % TQ
    n_active = (t0 + TQ + KCH - 1) // KCH

    for g in range(NSA_KV_HEADS):
        kc = kc_ref[:, g * LANES:(g + 1) * LANES]
        vc = vc_ref[:, g * LANES:(g + 1) * LANES]
        qs = _stack_heads(q_ref, g, halves)
        s = jnp.where(cmask, _dot_nt(qs, kc), NEG)
        e = jnp.exp(s - jnp.max(s, axis=-1, keepdims=True))
        p = jnp.where(cmask, e * (1.0 / jnp.sum(e, axis=-1, keepdims=True)), 0.0)
        _unstack_store(ocmp_ref, _dot(p.astype(BF16), vc), g, halves)
        psum = p[0:TQ]
        for h in range(1, NSA_GROUP):
            psum = psum + p[h * TQ:(h + 1) * TQ]

        imp = _split_dot(psum, ov_ref[...], passes=3)
        jidx = _lane_iota(imp.shape)
        qblk = (t0 + _row_iota(imp.shape)) // SEL_BLOCK
        score = jnp.where(jidx > qblk, -jnp.inf,
                          jnp.where((jidx == 0) | (jidx > qblk - N_LOCAL_FORCED), jnp.inf, imp))
        sel_t = _topk_mask_t(score.T[0:n_blocks], n_blocks, N_SEL)
        sel_t = jnp.concatenate([sel_t, jnp.zeros((imp.shape[1] - n_blocks, TQ), F32)], axis=0)
        sel = sel_t.T.astype(BF16)
        for c in range(n_keys // KCH):
            selm_ref[c] = _dot(sel, ex_ref[:, c * KCH:(c + 1) * KCH])

        k_rows = slice(g * HEAD_DIM, (g + 1) * HEAD_DIM)
        v_rows = slice(LANES + g * HEAD_DIM, LANES + (g + 1) * HEAD_DIM)

        def fold(x, op):
            out = x[:, 0:LANES]
            for u in range(1, KCH // LANES):
                out = op(out, x[:, u * LANES:(u + 1) * LANES])
            return out

        def scores(c, mrun):
            kt = _twice(jnp.concatenate([slc_ref[0, c * KCH_CHUNKS + u, k_rows, :]
                                         for u in range(KCH_CHUNKS)], axis=1))
            sel = jnp.concatenate([selm_ref[c]] * NSA_GROUP, axis=0)
            valid = (sel > 0.5) & ((c * KCH + _lane_iota((rows, KCH))) <= t_k)
            s = jnp.where(valid, _dot(qs, kt), NEG)
            s_ref[c] = s
            return jnp.maximum(mrun, fold(s, jnp.maximum))

        mrun = lax.fori_loop(0, n_active, scores, jnp.full((rows, LANES), NEG, F32))
        m = jnp.max(mrun, axis=-1, keepdims=True)

        def values(c, carry):
            lrun, acc = carry
            vt = _twice(jnp.concatenate([slc_ref[0, c * KCH_CHUNKS + u, v_rows, :]
                                         for u in range(KCH_CHUNKS)], axis=1))
            e = jnp.exp(s_ref[c] - m)
            return lrun + fold(e, jnp.add), acc + _dot_nt(e.astype(BF16), vt)

        zero = jnp.zeros((rows, LANES), F32)
        lrun, acc = lax.fori_loop(0, n_active, values, (zero, zero))
        _unstack_store(oslc_ref, acc / jnp.sum(lrun, axis=-1, keepdims=True), g, halves)


def _nsa_prompt(q, kc, vc, slcb, batch, seq):
    n_cmp = seq // CHUNK
    n_valid = (seq - CMP_BLOCK) // CMP_STRIDE + 1
    n_blocks = seq // SEL_BLOCK
    ov = _overlap_matrix(n_valid, n_cmp, n_blocks, LANES)
    ex = _expand_matrix(LANES, seq)
    nq = seq // TQ
    per_b = lambda rows, w: pl.BlockSpec((rows, w), lambda b, i: (b, 0))
    qspec = pl.BlockSpec((TQ, Q_NSA_W), lambda b, i: (b * nq + i, 0))
    const = lambda a: pl.BlockSpec(a.shape, lambda b, i: (0, 0))
    return pl.pallas_call(
        _nsa_prompt_kernel,
        grid=(batch, nq),
        in_specs=[qspec, per_b(n_cmp, 2 * LANES), per_b(n_cmp, 2 * LANES),
                  pl.BlockSpec((1,) + slcb.shape[1:], lambda b, i: (b, 0, 0, 0)), const(ov), const(ex)],
        out_specs=[qspec, qspec],
        out_shape=[jax.ShapeDtypeStruct((batch * seq, Q_NSA_W), F32)] * 2,
        scratch_shapes=[pltpu.VMEM((seq // KCH, TQ, KCH), F32),
                        pltpu.VMEM((seq // KCH, NSA_GROUP * TQ, KCH), F32)],
        compiler_params=pltpu.CompilerParams(dimension_semantics=("parallel", "arbitrary"),
                                             vmem_limit_bytes=VMEM_LIMIT),
        name="nsa_prompt",
    )(q, kc, vc, slcb, ov, ex)


WSPAN = WINDOW + TQ
WSPAN_CHUNKS = WSPAN // KT


def _win_prompt_kernel(q_ref, kv_ref, o_ref):
    t0 = pl.program_id(1) * TQ
    n_chunks = kv_ref.shape[1]
    c0 = jnp.clip(t0 // KT - WINDOW // KT, 0, n_chunks - WSPAN_CHUNKS)
    rows = NSA_GROUP * TQ
    t = t0 + _row_iota((rows, WSPAN)) % TQ
    kpos = c0 * KT + _lane_iota((rows, WSPAN))
    valid = (kpos <= t) & (kpos >= t - WINDOW)
    halves = _half_masks((TQ, LANES))
    for g in range(NSA_KV_HEADS):
        kt = _twice(jnp.concatenate([kv_ref[0, c0 + u, g * HEAD_DIM:(g + 1) * HEAD_DIM, :]
                                     for u in range(WSPAN_CHUNKS)], axis=1))
        vt = _twice(jnp.concatenate([kv_ref[0, c0 + u, LANES + g * HEAD_DIM:LANES + (g + 1) * HEAD_DIM, :]
                                     for u in range(WSPAN_CHUNKS)], axis=1))
        s = jnp.where(valid, _dot(_stack_heads(q_ref, g, halves), kt), NEG)
        e = jnp.exp(s - jnp.max(s, axis=-1, keepdims=True))
        o = _dot_nt(e.astype(BF16), vt) / jnp.sum(e, axis=-1, keepdims=True)
        _unstack_store(o_ref, o, g, halves)


def _win_prompt(q, winb, batch, seq):
    nq = seq // TQ
    qspec = pl.BlockSpec((TQ, Q_NSA_W), lambda b, i: (b * nq + i, 0))
    return pl.pallas_call(
        _win_prompt_kernel,
        grid=(batch, nq),
        in_specs=[qspec, pl.BlockSpec((1,) + winb.shape[1:], lambda b, i: (b, 0, 0, 0))],
        out_specs=qspec,
        out_shape=jax.ShapeDtypeStruct((batch * seq, Q_NSA_W), F32),
        compiler_params=pltpu.CompilerParams(dimension_semantics=("parallel", "arbitrary"),
                                             vmem_limit_bytes=VMEM_LIMIT),
        name="win_prompt",
    )(q, winb)


SB_TQ = 256
SB_KB = 256
SB_KB_CHUNKS = SB_KB // KT
F32_EXP_ZERO_BELOW = -104.0


def _suffix_matrix(n, n_total_cols):
    j = np.arange(n)[:, None]
    s = np.arange(n)[None, :]
    return jnp.asarray(np.concatenate([j >= s, np.ones((n, n_total_cols), bool)], axis=1), BF16)


def _log1m_sigmoid(z):
    return -jnp.maximum(z, 0.0) - jnp.log1p(jnp.exp(-jnp.abs(z)))


def _sb_prompt_kernel(q_ref, k_ref, v_ref, u_ref, o_ref):
    i = pl.program_id(2)
    halves = _half_masks((SB_TQ, LANES))
    q128 = q_ref[...]
    qms = [jnp.where(h, q128, jnp.zeros_like(q128)) for h in halves]
    u = u_ref[...]
    zero = jnp.zeros((SB_TQ, LANES), F32)

    def block(kb, carry, diagonal):
        kt = jnp.concatenate([k_ref[0, kb * SB_KB_CHUNKS + u_, :, :] for u_ in range(SB_KB_CHUNKS)], axis=1)
        vt = jnp.concatenate([v_ref[0, kb * SB_KB_CHUNKS + u_, :, :] for u_ in range(SB_KB_CHUNKS)], axis=1)
        if diagonal:
            mask = _lane_iota((SB_TQ, SB_KB)) < _row_iota((SB_TQ, SB_KB))
        out = []
        for hh in range(2):
            c, acc = carry[hh]
            z = _dot(qms[hh], kt)
            l1m = _log1m_sigmoid(z)
            if diagonal:
                l1m = jnp.where(mask, l1m, 0.0)
            res = _split_dot(l1m, u, passes=2)
            c_full = jnp.concatenate([c] * (SB_KB // LANES), axis=1)
            a = jnp.exp(z + res[:, :SB_KB] + c_full)
            if diagonal:
                a = jnp.where(mask, a, 0.0)
            out.append((c + res[:, SB_KB:], acc + _dot_nt(a.astype(BF16), vt)))
        return tuple(out)

    carry = block(i, ((zero, zero), (zero, zero)), True)

    def cond(state):
        n, go = state[0], state[1]
        return (n <= i) & go

    def body(state):
        n = state[0]
        carry = block(i - n, state[2], False)
        cmax = jnp.max(jnp.maximum(carry[0][0], carry[1][0]))
        return n + 1, cmax >= F32_EXP_ZERO_BELOW, carry

    _, _, carry = lax.while_loop(cond, body, (jnp.int32(1), jnp.bool_(True), carry))
    o_ref[...] = jnp.where(halves[0], carry[0][1], carry[1][1])


def _sb_prompt(qsb, sbb, batch, seq):
    nq = seq // SB_TQ
    npair = SB_HEADS // 2
    u = _suffix_matrix(SB_KB, LANES)
    n_chunks = sbb.shape[1]
    kspec = pl.BlockSpec((1, n_chunks, LANES, KT), lambda b, j, i: (b, 0, j, 0))
    vspec = pl.BlockSpec((1, n_chunks, LANES, KT), lambda b, j, i: (b, 0, npair + j, 0))
    qspec = pl.BlockSpec((SB_TQ, LANES), lambda b, j, i: (b * nq + i, j))
    return pl.pallas_call(
        _sb_prompt_kernel,
        grid=(batch, npair, nq),
        in_specs=[qspec, kspec, vspec, pl.BlockSpec(u.shape, lambda b, j, i: (0, 0))],
        out_specs=qspec,
        out_shape=jax.ShapeDtypeStruct((batch * seq, SB_QW), F32),
        compiler_params=pltpu.CompilerParams(dimension_semantics=("parallel", "parallel", "arbitrary"),
                                             vmem_limit_bytes=VMEM_LIMIT),
        name="sb_prompt",
    )(qsb, sbb, sbb, u)


N_FF_CHUNKS = 2
FF_CHUNK = D_FF // N_FF_CHUNKS


def _gate_expand_matrices():
    m = np.zeros((3, LANES, Q_NSA_W), np.float32)
    for b in range(3):
        for h in range(NSA_HEADS):
            m[b, b * NSA_HEADS + h, h * HEAD_DIM:(h + 1) * HEAD_DIM] = 1.0
    return jnp.asarray(m, BF16)


def _head_rms(o, gain, bd):
    cols = []
    for c in range(o.shape[1] // LANES):
        oc = o[:, c * LANES:(c + 1) * LANES]
        ms = _split_dot(oc * oc, bd, passes=2)
        cols.append(oc * lax.rsqrt(ms + EPS))
    return jnp.concatenate(cols, axis=1) * gain


def _finish_kernel(x_ref, gates_ref, ocmp_ref, oslc_ref, owin_ref, osb_ref, gexp_ref, gon_ref, gos_ref,
                   bd_ref, wout_ref, gffn_ref, wg_ref, wu_ref, wd_ref, y_ref, hn_ref):
    @pl.when(pl.program_id(1) == 0)
    def _():
        gates = gates_ref[...]
        bd = bd_ref[...]
        o_nsa = (_split_dot(gates, gexp_ref[0]) * ocmp_ref[...]
                 + _split_dot(gates, gexp_ref[1]) * oslc_ref[...]
                 + _split_dot(gates, gexp_ref[2]) * owin_ref[...])
        mix = jnp.concatenate([_head_rms(o_nsa, gon_ref[...], bd), _head_rms(osb_ref[...], gos_ref[...], bd)],
                              axis=1)
        h = x_ref[...] + _dot(mix.astype(BF16), wout_ref[...])
        hn = h * lax.rsqrt(jnp.mean(h * h, axis=-1, keepdims=True) + EPS) * gffn_ref[...]
        hn_ref[...] = hn.astype(BF16)
        y_ref[...] = h

    hn = hn_ref[...]
    gate = _dot(hn, wg_ref[...])
    up = _dot(hn, wu_ref[...])
    act = gate * jax.nn.sigmoid(gate) * up
    y_ref[...] += _dot(act.astype(BF16), wd_ref[...])


def _finish(x2d, gates, o_cmp, o_slc, o_win, o_sb, tm, g_on, g_os, w_out, g_ffn, w_gate, w_up, w_down):
    n = x2d.shape[0]
    gexp = _gate_expand_matrices()
    bd = _head_mean_matrix()
    gon = g_on.reshape(1, Q_NSA_W)
    gos = g_os.reshape(1, SB_QW)
    gffn = g_ffn.reshape(1, D_MODEL)
    row = lambda w: pl.BlockSpec((tm, w), lambda i, c: (i, 0))
    const = lambda a: pl.BlockSpec(a.shape, lambda i, c: (0,) * a.ndim)
    return pl.pallas_call(
        _finish_kernel,
        grid=(n // tm, N_FF_CHUNKS),
        in_specs=[row(D_MODEL), row(LANES), row(Q_NSA_W), row(Q_NSA_W), row(Q_NSA_W), row(SB_QW),
                  const(gexp), const(gon), const(gos), const(bd), const(w_out), const(gffn),
                  pl.BlockSpec((D_MODEL, FF_CHUNK), lambda i, c: (0, c)),
                  pl.BlockSpec((D_MODEL, FF_CHUNK), lambda i, c: (0, c)),
                  pl.BlockSpec((FF_CHUNK, D_MODEL), lambda i, c: (c, 0))],
        out_specs=row(D_MODEL),
        out_shape=jax.ShapeDtypeStruct((n, D_MODEL), F32),
        scratch_shapes=[pltpu.VMEM((tm, D_MODEL), BF16)],
        compiler_params=pltpu.CompilerParams(dimension_semantics=("parallel", "arbitrary"),
                                             vmem_limit_bytes=VMEM_LIMIT),
        name="finish",
    )(x2d, gates, o_cmp, o_slc, o_win, o_sb, gexp, gon, gos, bd, w_out, gffn, w_gate, w_up, w_down)


ROWS8 = SUBLANES
PAGE = 128


def _decode_q(q_row, g):
    shape = (ROWS8, LANES)
    row = _row_iota(shape)
    lane = _lane_iota(shape)
    lo = 2 * g * LANES
    base = jnp.where(row < 2, jnp.broadcast_to(q_row[:, lo:lo + LANES], shape),
                     jnp.broadcast_to(q_row[:, lo + LANES:lo + 2 * LANES], shape))
    val = jnp.where((row % 2) == g, base, pltpu.roll(base, HEAD_DIM, 1))
    keep = (row < NSA_GROUP) & ((lane >= HEAD_DIM) == (g == 1))
    return jnp.where(keep, val, 0.0).astype(BF16)


def _decode_place(o, g):
    return jnp.where((_row_iota(o.shape) % 2) == g, o, pltpu.roll(o, HEAD_DIM, 1))


def _decode_store(o_ref, val, g):
    low = _lane_iota((1, LANES)) < HEAD_DIM
    for jp in range(2):
        lo = 2 * g * LANES + jp * LANES
        o_ref[0, 0:1, lo:lo + LANES] = jnp.where(low, val[2 * jp:2 * jp + 1], val[2 * jp + 1:2 * jp + 2])


def _decode_attend(qg, kt, vt, valid, knew, vnew):
    s = _dot(qg, kt)
    if valid is not None:
        s = jnp.where(valid, s, NEG)
    kn = jnp.broadcast_to(knew, (LANES, LANES)).astype(BF16)
    vn = jnp.broadcast_to(vnew, (LANES, LANES)).astype(BF16)
    s2 = jnp.where(_lane_iota((ROWS8, LANES)) == 0, _dot_nt(qg, kn), NEG)
    m = jnp.maximum(jnp.max(s, axis=-1, keepdims=True), jnp.max(s2, axis=-1, keepdims=True))
    e = jnp.exp(s - m)
    e2 = jnp.exp(s2 - m)
    l = jnp.sum(e, axis=-1, keepdims=True) + jnp.sum(e2, axis=-1, keepdims=True)
    return (_dot_nt(e.astype(BF16), vt) + _dot(e2.astype(BF16), vn)) / l


def _cmp_sample_kernel(pt_ref, q_ref, cache_ref, pe_ref, w1_ref, w2_ref, ov_ref,
                       ocmp_ref, imp_ref, stage_ref, kbuf_ref, vbuf_ref, sem):
    b = pl.program_id(0)
    n_pages = stage_ref.shape[1]
    slot = b % 2

    def page_copy(seq, slot_, p):
        return pltpu.make_async_copy(cache_ref.at[pt_ref[seq, p], pl.ds(0, 2 * LANES), :],
                                     stage_ref.at[slot_, p], sem.at[slot_])

    def start_pages(seq, slot_):
        for p in range(n_pages):
            page_copy(seq, slot_, p).start()

    @pl.when(b == 0)
    def _():
        start_pages(0, 0)

    @pl.when(b + 1 < pl.num_programs(0))
    def _():
        start_pages(b + 1, 1 - slot)

    for p in range(n_pages):
        page_copy(b, slot, p).wait()
    for p in range(n_pages):
        kbuf_ref[p * PAGE:(p + 1) * PAGE, :] = stage_ref[slot, p, 0:LANES, :].T
        vbuf_ref[p * PAGE:(p + 1) * PAGE, :] = stage_ref[slot, p, LANES:2 * LANES, :].T

    n_chunks = kbuf_ref.shape[0] // CHUNK
    kc = _compress_rows(kbuf_ref, n_chunks, 0, pe_ref, w1_ref, w2_ref).astype(BF16)
    vc = _compress_rows(vbuf_ref, n_chunks, 1, pe_ref, w1_ref, w2_ref).astype(BF16)
    q_row = q_ref[0]
    cmask = _lane_iota((ROWS8, n_chunks)) < (n_chunks - 1)
    ocmp_ref[...] = jnp.zeros_like(ocmp_ref)
    imp_ref[...] = jnp.zeros_like(imp_ref)
    for g in range(NSA_KV_HEADS):
        qg = _decode_q(q_row, g)
        s = jnp.where(cmask, _dot_nt(qg, kc[:, g * LANES:(g + 1) * LANES]), NEG)
        e = jnp.exp(s - jnp.max(s, axis=-1, keepdims=True))
        p = jnp.where(cmask, e / jnp.sum(e, axis=-1, keepdims=True), 0.0)
        o = _dot(p.astype(BF16), vc[:, g * LANES:(g + 1) * LANES])
        _decode_store(ocmp_ref, _decode_place(o, g), g)
        psum = jnp.sum(jnp.where(_row_iota(p.shape) < NSA_GROUP, p, 0.0), axis=0, keepdims=True)
        imp = _split_dot(jnp.broadcast_to(psum, p.shape), ov_ref[...], passes=3)
        imp_ref[0, g:g + 1, :] = imp[0:1]


def _cmp_sample(page_table, q3, cache_nsa_t, pe, w1, w2, past_len):
    db, n_pages = page_table.shape
    n_chunks = past_len // CHUNK
    n_blocks = past_len // SEL_BLOCK + 1
    ncol = 2 * LANES
    assert n_blocks <= ncol
    ov = _overlap_matrix(n_chunks - 1, n_chunks, n_blocks, ncol)
    const = lambda a: pl.BlockSpec(a.shape, lambda b, pt: (0,) * a.ndim)
    gs = pltpu.PrefetchScalarGridSpec(
        num_scalar_prefetch=1,
        grid=(db,),
        in_specs=[pl.BlockSpec((1, 1, Q_NSA_W), lambda b, pt: (b, 0, 0)),
                  pl.BlockSpec(memory_space=pl.ANY), const(pe), const(w1), const(w2), const(ov)],
        out_specs=[pl.BlockSpec((1, ROWS8, Q_NSA_W), lambda b, pt: (b, 0, 0)),
                   pl.BlockSpec((1, ROWS8, ncol), lambda b, pt: (b, 0, 0))],
        scratch_shapes=[pltpu.VMEM((2, n_pages, 2 * LANES, PAGE), F32),
                        pltpu.VMEM((past_len, LANES), F32), pltpu.VMEM((past_len, LANES), F32),
                        pltpu.SemaphoreType.DMA((2,))],
    )
    return pl.pallas_call(
        _cmp_sample_kernel,
        grid_spec=gs,
        out_shape=[jax.ShapeDtypeStruct((db, ROWS8, Q_NSA_W), F32), jax.ShapeDtypeStruct((db, ROWS8, ncol), F32)],
        compiler_params=pltpu.CompilerParams(dimension_semantics=("arbitrary",), vmem_limit_bytes=VMEM_LIMIT),
        name="cmp_sample",
    )(page_table, q3, cache_nsa_t, pe, w1, w2, ov)


def _topk_sample_kernel(imp_ref, idx_ref, *, n_blocks, q_block):
    imp = imp_ref[...]
    jidx = _lane_iota(imp.shape)
    score = jnp.where(jidx > q_block, -jnp.inf,
                      jnp.where((jidx == 0) | (jidx > q_block - N_LOCAL_FORCED), jnp.inf, imp))
    nb_pad = -(-n_blocks // SUBLANES) * SUBLANES
    st = score.T[0:nb_pad]
    jrow = _row_iota(st.shape)
    st = jnp.where(jrow < n_blocks, st, -jnp.inf)
    cnt = jnp.zeros(st.shape, F32)
    for i in range(n_blocks):
        ri = st[i:i + 1, :]
        cnt += jnp.where((ri > st) | ((ri == st) & (i < jrow)), 1.0, 0.0)
    jf = jrow.astype(F32)
    rows = []
    for k in range(N_SEL):
        hit = (cnt == k) & (st > -jnp.inf)
        rows.append(jnp.sum(jnp.where(hit, jf, 0.0), axis=0, keepdims=True)
                    + jnp.where(jnp.sum(jnp.where(hit, 1.0, 0.0), axis=0, keepdims=True) > 0.5, 0.0, -1.0))
    idx_ref[...] = jnp.concatenate(rows, axis=0).astype(jnp.int32)


def _topk_sample(imp2d, n_blocks, q_block):
    rows = imp2d.shape[0]
    return pl.pallas_call(
        functools.partial(_topk_sample_kernel, n_blocks=n_blocks, q_block=q_block),
        out_shape=jax.ShapeDtypeStruct((N_SEL, rows), jnp.int32),
        compiler_params=pltpu.CompilerParams(vmem_limit_bytes=VMEM_LIMIT),
        name="topk_sample",
    )(imp2d)


def _slc_sample_kernel(pt_ref, idx_ref, q_ref, new_ref, cache_ref, o_ref, buf_ref, sem, *, n_past_blocks):
    b = pl.program_id(0)
    per_page = PAGE // SEL_BLOCK

    buf_slot = b % 2

    def block_index(seq, g, k):
        return idx_ref[(seq * NSA_KV_HEADS + g) * N_SEL + k]

    def page_copy(seq, slot_, g, k):
        j = jnp.clip(block_index(seq, g, k), 0, n_past_blocks - 1)
        return pltpu.make_async_copy(cache_ref.at[pt_ref[seq, j // per_page], pl.ds(2 * LANES, 2 * LANES), :],
                                     buf_ref.at[slot_, g, k], sem.at[slot_])

    def start_pages(seq, slot_):
        for g in range(NSA_KV_HEADS):
            for k in range(N_SEL):
                page_copy(seq, slot_, g, k).start()

    @pl.when(b == 0)
    def _():
        start_pages(0, 0)

    @pl.when(b + 1 < pl.num_programs(0))
    def _():
        start_pages(b + 1, 1 - buf_slot)

    for g in range(NSA_KV_HEADS):
        for k in range(N_SEL):
            page_copy(b, buf_slot, g, k).wait()

    q_row = q_ref[0]
    new = new_ref[0]
    o_ref[...] = jnp.zeros_like(o_ref)
    n_keys = N_SEL * PAGE
    lane = _lane_iota((ROWS8, n_keys))
    slot = lane // PAGE
    half = (lane % PAGE) // SEL_BLOCK
    for g in range(NSA_KV_HEADS):
        valid = jnp.zeros((ROWS8, n_keys), jnp.bool_)
        for k in range(N_SEL):
            j = block_index(b, g, k)
            valid = valid | ((slot == k) & (half == j % per_page) & ((j >= 0) & (j < n_past_blocks)))
        kt = jnp.concatenate([buf_ref[buf_slot, g, k, 0:LANES, :] for k in range(N_SEL)], axis=1).astype(BF16)
        vt = jnp.concatenate([buf_ref[buf_slot, g, k, LANES:2 * LANES, :] for k in range(N_SEL)],
                             axis=1).astype(BF16)
        o = _decode_attend(_decode_q(q_row, g), kt, vt, valid,
                           new[:, 2 * LANES:3 * LANES], new[:, 3 * LANES:4 * LANES])
        _decode_store(o_ref, _decode_place(o, g), g)


def _slc_sample(page_table, idx_flat, q3, nsa_new3, cache_nsa_t, past_len):
    db = page_table.shape[0]
    gs = pltpu.PrefetchScalarGridSpec(
        num_scalar_prefetch=2,
        grid=(db,),
        in_specs=[pl.BlockSpec((1, 1, Q_NSA_W), lambda b, pt, ix: (b, 0, 0)),
                  pl.BlockSpec((1, 1, 4 * LANES), lambda b, pt, ix: (b, 0, 0)),
                  pl.BlockSpec(memory_space=pl.ANY)],
        out_specs=pl.BlockSpec((1, ROWS8, Q_NSA_W), lambda b, pt, ix: (b, 0, 0)),
        scratch_shapes=[pltpu.VMEM((2, NSA_KV_HEADS, N_SEL, 2 * LANES, PAGE), F32),
                        pltpu.SemaphoreType.DMA((2,))],
    )
    return pl.pallas_call(
        functools.partial(_slc_sample_kernel, n_past_blocks=past_len // SEL_BLOCK),
        grid_spec=gs,
        out_shape=jax.ShapeDtypeStruct((db, ROWS8, Q_NSA_W), F32),
        compiler_params=pltpu.CompilerParams(dimension_semantics=("arbitrary",), vmem_limit_bytes=VMEM_LIMIT),
        name="slc_sample",
    )(page_table, idx_flat, q3, nsa_new3, cache_nsa_t)


def _win_sample_kernel(q_ref, new_ref, win_ref, o_ref):
    q_row = q_ref[0]
    new = new_ref[0]
    o_ref[...] = jnp.zeros_like(o_ref)
    for g in range(NSA_KV_HEADS):
        o = _decode_attend(_decode_q(q_row, g), win_ref[0, 0:LANES, :].astype(BF16),
                           win_ref[0, LANES:2 * LANES, :].astype(BF16), None,
                           new[:, 0:LANES], new[:, LANES:2 * LANES])
        _decode_store(o_ref, _decode_place(o, g), g)


def _win_sample(q3, win_new3, state_win_t):
    db, w, wb = state_win_t.shape
    return pl.pallas_call(
        _win_sample_kernel,
        grid=(db,),
        in_specs=[pl.BlockSpec((1, 1, Q_NSA_W), lambda b: (b, 0, 0)),
                  pl.BlockSpec((1, 1, w), lambda b: (b, 0, 0)),
                  pl.BlockSpec((1, w, wb), lambda b: (b, 0, 0))],
        out_specs=pl.BlockSpec((1, ROWS8, Q_NSA_W), lambda b: (b, 0, 0)),
        out_shape=jax.ShapeDtypeStruct((db, ROWS8, Q_NSA_W), F32),
        compiler_params=pltpu.CompilerParams(dimension_semantics=("parallel",), vmem_limit_bytes=VMEM_LIMIT),
        name="win_sample",
    )(q3, win_new3, state_win_t)


SB_GROUP_PAGES = 8
SB_GROUP_ROWS = SB_GROUP_PAGES * PAGE


def _sb_sample_kernel(pt_ref, q_ref, cache_ref, u_ref, o_ref, buf_ref, sem):
    b = pl.program_id(0)
    n_groups = pt_ref.shape[1] // SB_GROUP_PAGES

    has_next = b + 1 < pl.num_programs(0)
    last = n_groups - 1
    assert n_groups % 2 == 0

    def page_copy(seq, grp, slot, p):
        return pltpu.make_async_copy(cache_ref.at[pt_ref[seq, grp * SB_GROUP_PAGES + p]],
                                     buf_ref.at[slot, p], sem.at[slot])

    def start_group(seq, grp, slot):
        for p in range(SB_GROUP_PAGES):
            page_copy(seq, grp, slot, p).start()

    def wait_group(grp, slot):
        for p in range(SB_GROUP_PAGES):
            page_copy(b, grp, slot, p).wait()

    shape = (ROWS8, SB_QW)
    head_lanes = _row_iota(shape) == (_lane_iota(shape) // HEAD_DIM)
    qbd = jnp.where(head_lanes, jnp.broadcast_to(q_ref[0], shape), 0.0).astype(BF16)
    u = u_ref[...]
    n_ch = SB_GROUP_ROWS // LANES

    @pl.when(b == 0)
    def _():
        start_group(0, last, last % 2)

    def body(n, carry):
        c, acc = carry
        grp = last - n
        slot = grp % 2

        @pl.when(grp > 0)
        def _():
            start_group(b, grp - 1, 1 - slot)

        @pl.when((grp == 0) & has_next)
        def _():
            start_group(b + 1, last, last % 2)

        wait_group(grp, slot)
        kt = jnp.concatenate([buf_ref[slot, p, 0:SB_QW, :] for p in range(SB_GROUP_PAGES)], axis=1).astype(BF16)
        vt = jnp.concatenate([buf_ref[slot, p, SB_QW:2 * SB_QW, :] for p in range(SB_GROUP_PAGES)],
                             axis=1).astype(BF16)
        z = _dot(qbd, kt)
        l1m = _log1m_sigmoid(z)
        stacked = jnp.concatenate([l1m[:, i * LANES:(i + 1) * LANES] for i in range(n_ch)], axis=0)
        res = _split_dot(stacked, u, passes=2)
        a_chunks = [None] * n_ch
        for i in reversed(range(n_ch)):
            sl = slice(i * LANES, (i + 1) * LANES)
            a_chunks[i] = jnp.exp(z[:, sl] + res[i * ROWS8:(i + 1) * ROWS8, 0:LANES] + c)
            c = c + res[i * ROWS8:(i + 1) * ROWS8, LANES:2 * LANES]
        a = jnp.concatenate(a_chunks, axis=1).astype(BF16)
        return c, acc + _dot_nt(a, vt)

    c0 = jnp.zeros((ROWS8, LANES), F32)
    _, acc = lax.fori_loop(0, n_groups, body, (c0, jnp.zeros(shape, F32)))
    o_ref[...] = jnp.zeros_like(o_ref)
    o_ref[0, 0:1, :] = jnp.sum(jnp.where(head_lanes, acc, 0.0), axis=0, keepdims=True)


def _sb_sample(page_table, qsb3, cache_sb_t):
    db = page_table.shape[0]
    w = cache_sb_t.shape[1]
    u = _suffix_matrix(LANES, LANES)
    gs = pltpu.PrefetchScalarGridSpec(
        num_scalar_prefetch=1,
        grid=(db,),
        in_specs=[pl.BlockSpec((1, 1, SB_QW), lambda b, pt: (b, 0, 0)),
                  pl.BlockSpec(memory_space=pl.ANY),
                  pl.BlockSpec(u.shape, lambda b, pt: (0, 0))],
        out_specs=pl.BlockSpec((1, ROWS8, SB_QW), lambda b, pt: (b, 0, 0)),
        scratch_shapes=[pltpu.VMEM((2, SB_GROUP_PAGES, w, PAGE), F32), pltpu.SemaphoreType.DMA((2,))],
    )
    return pl.pallas_call(
        _sb_sample_kernel,
        grid_spec=gs,
        out_shape=jax.ShapeDtypeStruct((db, ROWS8, SB_QW), F32),
        compiler_params=pltpu.CompilerParams(dimension_semantics=("arbitrary",), vmem_limit_bytes=VMEM_LIMIT),
        name="sb_sample",
    )(page_table, qsb3, cache_sb_t, u)


PROMPT_TM = 256
FINISH_TM = 512


def _feature_major(x, n_lead):
    perm = tuple(range(n_lead)) + tuple(range(n_lead + 1, x.ndim)) + (n_lead,)
    xt = x.transpose(perm)
    return xt.reshape(xt.shape[:n_lead] + (-1, xt.shape[-1]))


def _token_major(xt, feat_shape):
    b, _, t = xt.shape
    nd = len(feat_shape)
    return xt.reshape((b,) + feat_shape + (t,)).transpose((0, nd + 1) + tuple(range(1, nd + 1)))


def kernel(x_prompt, x_sample, cache_nsa_kv, cache_sb_kv, state_nsa_window, page_table, g_attn, w_in, g_q, g_k,
           w_cmp1_k, w_cmp2_k, pe_cmp_k, w_cmp1_v, w_cmp2_v, pe_cmp_v, g_out_nsa, g_out_sb, w_out, g_ffn,
           w_gate_up, w_down):
    depth = w_in.shape[0]
    assert depth == 1
    batch, seq, _ = x_prompt.shape
    db, dseq, _ = x_sample.shape
    assert dseq == 1
    page = cache_nsa_kv.shape[2]
    past_len = page_table.shape[1] * page
    wb = state_nsa_window.shape[2]
    assert page == PAGE and wb == WINDOW

    l = 0
    w_packed = _pack_w_in(w_in[l])
    pe, w1, w2 = _pack_cmp_weights(w_cmp1_k[l], w_cmp2_k[l], pe_cmp_k[l], w_cmp1_v[l], w_cmp2_v[l], pe_cmp_v[l])
    fin_w = (g_out_nsa[l], g_out_sb[l], w_out[l].astype(BF16), g_ffn[l],
             w_gate_up[l][:, :D_FF].astype(BF16), w_gate_up[l][:, D_FF:].astype(BF16), w_down[l].astype(BF16))

    xp = x_prompt.reshape(batch * seq, D_MODEL)
    (q, gates, cmp_rows, qsb, nsa_t, win_t, sbkv_t, slcb, winb, sbb) = _project_prompt(
        xp, batch, seq, PROMPT_TM, g_attn[l], w_packed, g_q[l], g_k[l])
    kc, vc = _compress_prompt(cmp_rows, seq, pe, w1, w2)
    o_cmp, o_slc = _nsa_prompt(q, kc, vc, slcb, batch, seq)
    o_win = _win_prompt(q, winb, batch, seq)
    o_sb = _sb_prompt(qsb, sbb, batch, seq)
    yp = _finish(xp, gates, o_cmp, o_slc, o_win, o_sb, FINISH_TM, *fin_w)

    xs = x_sample.reshape(db, D_MODEL)
    q_s, gates_s, nsa_s, win_s, qsb_s, sbkv_s = _project_sample(xs, past_len, g_attn[l], w_packed, g_q[l], g_k[l])
    q3 = q_s.reshape(db, 1, Q_NSA_W)
    cache_nsa_t = _feature_major(cache_nsa_kv[l], 1)
    cache_sb_t = _feature_major(cache_sb_kv[l], 1)
    s_win_t = _feature_major(state_nsa_window[l], 1)
    o_cmp_s, imp = _cmp_sample(page_table, q3, cache_nsa_t, pe, w1, w2, past_len)
    n_blocks = past_len // SEL_BLOCK + 1
    idx = _topk_sample(imp.reshape(db * ROWS8, imp.shape[-1]), n_blocks, past_len // SEL_BLOCK)
    idx = idx.reshape(N_SEL, db, ROWS8)[:, :, :NSA_KV_HEADS].transpose(1, 2, 0).reshape(-1)
    o_slc_s = _slc_sample(page_table, idx, q3, nsa_s.reshape(db, 1, 4 * LANES), cache_nsa_t, past_len)
    o_win_s = _win_sample(q3, win_s.reshape(db, 1, 2 * LANES), s_win_t)
    o_sb_s = _sb_sample(page_table, qsb_s.reshape(db, 1, SB_QW), cache_sb_t)
    ys = _finish(xs, gates_s, o_cmp_s[:, 0], o_slc_s[:, 0], o_win_s[:, 0], o_sb_s[:, 0], db, *fin_w)

    kvh = (NSA_KV_HEADS, HEAD_DIM)
    keep = min(WINDOW, seq)
    win_all_t = jnp.concatenate([s_win_t[:, :, dseq:], win_s.reshape(db, dseq, 2 * LANES).transpose(0, 2, 1)], axis=2)
    return (yp.reshape(batch, seq, D_MODEL), ys.reshape(db, dseq, D_MODEL),
            _token_major(nsa_t, (4,) + kvh)[None], nsa_s.reshape((depth, db, dseq, 4) + kvh),
            _token_major(sbkv_t, (2, SB_HEADS, HEAD_DIM))[None],
            sbkv_s.reshape(depth, db, dseq, 2, SB_HEADS, HEAD_DIM),
            _token_major(win_t[:, :, seq - keep:], (2,) + kvh)[None],
            _token_major(win_all_t, (2,) + kvh)[None])
```
